```python
import math
import jax, jax.numpy as jnp
from jax import lax
import numpy as np

D_MODEL = 1024
BATCH = 2
SEQ = 8192
DEPTH = 1
DEC_BATCH = 16
DEC_SEQ = 2048
PAST_LEN = 128

ATTN_HEAD_DIM = 64
DILATED_PATTERNS = ((128, 1), (512, 4), (2048, 16))
N_ATTN_GROUPS = len(DILATED_PATTERNS)
ATTN_HEADS_PER_GROUP = D_MODEL // 128
ATTN_HEADS = N_ATTN_GROUPS * ATTN_HEADS_PER_GROUP
ATTN_QKV_WIDTH = ATTN_HEADS * ATTN_HEAD_DIM
ATTN_OUT_WIDTH = ATTN_HEADS_PER_GROUP * ATTN_HEAD_DIM
ATTN_BLOCK = 64
ROPE_THETA = 10000.0
SSM_EXPAND = 2
D_INNER = SSM_EXPAND * D_MODEL
SSM_HEAD_DIM = 64
SSM_HEADS = D_INNER // SSM_HEAD_DIM
SSM_GROUPS = 8
SSM_HEADS_PER_GROUP = SSM_HEADS // SSM_GROUPS
D_STATE = 128
CONV_WIDTH = 5
CONV_DIM = D_INNER + 2 * SSM_GROUPS * D_STATE
SSD_CHUNK = 128
D_FF = 4 * D_MODEL
RMS_EPS = 1e-6
NEG_INF = -1e30
SPLIT_SIZES = (ATTN_QKV_WIDTH, ATTN_QKV_WIDTH, ATTN_QKV_WIDTH, D_INNER, CONV_DIM, 2 * SSM_HEADS, D_MODEL, D_MODEL)
SPLIT_POINTS = tuple(sum(SPLIT_SIZES[:i + 1]) for i in range(len(SPLIT_SIZES) - 1))
D_IN_PROJ = sum(SPLIT_SIZES)

kernel_name = 'hybrid_dilated_attn_bissd_encoder'


def rmsnorm(x, w):
    xf = x.astype(jnp.float32)
    y = xf * lax.rsqrt(jnp.mean(xf * xf, axis=-1, keepdims=True) + RMS_EPS)
    return (y * w.astype(jnp.float32)).astype(x.dtype)


def rope(x):
    s, hd = x.shape[1], x.shape[-1]
    half = hd // 2
    inv = ROPE_THETA ** (-(jnp.arange(half, dtype=jnp.float32) * 2.0 / hd))
    ang = jnp.arange(s, dtype=jnp.float32)[:, None] * inv[None, :]
    cos = jnp.cos(ang)[None, :, None, :]
    sin = jnp.sin(ang)[None, :, None, :]
    xf = x.astype(jnp.float32)
    x1, x2 = xf[..., :half], xf[..., half:]
    return jnp.concatenate([x1 * cos - x2 * sin, x2 * cos + x1 * sin], axis=-1).astype(x.dtype)


def to_classes(t, d):
    b, s, h, hd = t.shape
    return t.reshape(b, s // d, d, h, hd).transpose(0, 2, 1, 3, 4).reshape(b * d, s // d, h, hd)


def band_attention(q, k, v, radius):
    n, l, h, hd = q.shape
    nblk = -(-l // ATTN_BLOCK)
    p = nblk * ATTN_BLOCK
    span = ATTN_BLOCK + 2 * radius
    qp = jnp.pad(q, ((0, 0), (0, p - l), (0, 0), (0, 0))).reshape(n, nblk, ATTN_BLOCK, h, hd)
    kpad = ((0, 0), (radius, p - l + radius), (0, 0), (0, 0))
    kp = jnp.pad(k, kpad)
    vp = jnp.pad(v, kpad)
    idx = jnp.arange(nblk)[:, None] * ATTN_BLOCK + jnp.arange(span)[None, :]
    kb = kp[:, idx]
    vb = vp[:, idx]
    scores = jnp.einsum('nbqhd,nbkhd->nbhqk', qp, kb).astype(jnp.float32) * (1.0 / math.sqrt(hd))
    qpos = jnp.arange(nblk)[:, None] * ATTN_BLOCK + jnp.arange(ATTN_BLOCK)[None, :]
    kpos = idx - radius
    valid = (jnp.abs(qpos[:, :, None] - kpos[:, None, :]) <= radius) & ((kpos >= 0) & (kpos < l))[:, None, :]
    scores = jnp.where(valid[None, :, None, :, :], scores, NEG_INF)
    lse = jax.nn.logsumexp(scores, axis=-1)
    probs = jnp.exp(scores - lse[..., None])
    out = jnp.einsum('nbhqk,nbkhd->nbqhd', probs.astype(v.dtype), vb).reshape(n, p, h, hd)[:, :l]
    lse = lse.transpose(0, 1, 3, 2).reshape(n, p, h)[:, :l]
    return out, lse


def dilated_mixture_attention(q, k, v):
    b, s, _, hd = q.shape
    hg = ATTN_HEADS_PER_GROUP
    outs, lses = [], []
    for g, (window, dil) in enumerate(DILATED_PATTERNS):
        radius = window // (2 * dil)
        sl = slice(g * hg, (g + 1) * hg)
        o, lse = band_attention(to_classes(q[:, :, sl], dil), to_classes(k[:, :, sl], dil),
                                to_classes(v[:, :, sl], dil), radius)
        outs.append(o.reshape(b, dil, s // dil, hg, hd).transpose(0, 2, 1, 3, 4).reshape(b, s, hg, hd))
        lses.append(lse.reshape(b, dil, s // dil, hg).transpose(0, 2, 1, 3).reshape(b, s, hg))
    alpha = jax.nn.softmax(jnp.stack(lses, axis=0), axis=0)
    o = jnp.sum(alpha[..., None] * jnp.stack(outs, axis=0).astype(jnp.float32), axis=0)
    return o.astype(v.dtype)


def depthwise_conv(x, w, bias):
    c = x.shape[-1]
    y = lax.conv_general_dilated(x, w[:, None, :], window_strides=(1,),
                                 padding=[(CONV_WIDTH // 2, CONV_WIDTH // 2)],
                                 dimension_numbers=('NWC', 'WIO', 'NWC'), feature_group_count=c)
    return y + bias


def ssd_scan(xs, dt, a, bm, cm):
    b, l, g, e, p = xs.shape
    n = bm.shape[-1]
    c = l // SSD_CHUNK
    L = SSD_CHUNK
    X = (xs * dt[..., None]).reshape(b, c, L, g, e, p)
    Bc = bm.astype(jnp.float32).reshape(b, c, L, g, n)
    Cc = cm.astype(jnp.float32).reshape(b, c, L, g, n)
    a_cs = jnp.cumsum((dt * a).reshape(b, c, L, g, e).transpose(0, 3, 4, 1, 2), axis=-1)
    tril = jnp.tril(jnp.ones((L, L), dtype=bool))
    seg = a_cs[..., :, None] - a_cs[..., None, :]
    lmat = jnp.exp(jnp.where(tril, seg, -jnp.inf))
    cb = jnp.einsum('bclgn,bcsgn->bcgls', Cc, Bc)
    w = jnp.einsum('bcgls,bgecls->bcgels', cb, lmat)
    y_diag = jnp.einsum('bcgels,bcsgep->bclgep', w, X)
    decay_states = jnp.exp(a_cs[..., -1:] - a_cs)
    states = jnp.einsum('bclgn,bgecl,bclgep->bcgepn', Bc, decay_states, X)
    chunk_decay = jnp.exp(a_cs[..., -1])

    def step(h, inp):
        st, dec = inp
        return h * dec[..., None, None] + st, h

    h0 = jnp.zeros((b, g, e, p, n), jnp.float32)
    _, prev = lax.scan(step, h0, (jnp.moveaxis(states, 1, 0), jnp.moveaxis(chunk_decay, -1, 0)))
    prev = jnp.moveaxis(prev, 0, 1)
    y_off = jnp.einsum('bclgn,bcgepn,bgecl->bclgep', Cc, prev, jnp.exp(a_cs))
    return (y_diag + y_off).reshape(b, l, g, e, p)


def mixer_block(u, w_in, conv_w, conv_b, dt_bias, a_log, d_skip, ssm_norm, w_attn_branch, w_ssm_branch, w_out):
    bsz, s, _ = u.shape
    proj = u @ w_in
    q, k, v, z, xbc, dt_raw, g_a, g_m = jnp.split(proj, SPLIT_POINTS, axis=-1)
    q = rope(q.reshape(bsz, s, ATTN_HEADS, ATTN_HEAD_DIM))
    k = rope(k.reshape(bsz, s, ATTN_HEADS, ATTN_HEAD_DIM))
    v = v.reshape(bsz, s, ATTN_HEADS, ATTN_HEAD_DIM)
    o_attn = dilated_mixture_attention(q, k, v).reshape(bsz, s, ATTN_OUT_WIDTH) @ w_attn_branch
    xbc = jax.nn.silu(depthwise_conv(xbc, conv_w, conv_b))
    xs, bm, cm = jnp.split(xbc, (D_INNER, D_INNER + SSM_GROUPS * D_STATE), axis=-1)
    xs = xs.astype(jnp.float32).reshape(bsz, s, SSM_GROUPS, SSM_HEADS_PER_GROUP, SSM_HEAD_DIM)
    bm = bm.reshape(bsz, s, SSM_GROUPS, D_STATE)
    cm = cm.reshape(bsz, s, SSM_GROUPS, D_STATE)
    dt = jax.nn.softplus(dt_raw.astype(jnp.float32).reshape(bsz, s, 2, SSM_HEADS) + dt_bias.astype(jnp.float32))
    dt = dt.reshape(bsz, s, 2, SSM_GROUPS, SSM_HEADS_PER_GROUP)
    a = -jnp.exp(a_log.astype(jnp.float32)).reshape(2, SSM_GROUPS, SSM_HEADS_PER_GROUP)
    rev = lambda t: jnp.flip(t, axis=1)
    y_f = ssd_scan(xs, dt[:, :, 0], a[0], bm, cm)
    y_b = rev(ssd_scan(rev(xs), rev(dt[:, :, 1]), a[1], rev(bm), rev(cm)))
    d = d_skip.astype(jnp.float32).reshape(SSM_GROUPS, SSM_HEADS_PER_GROUP, 1)
    y = (y_f + y_b + d * xs).reshape(bsz, s, D_INNER)
    y = rmsnorm(y * jax.nn.silu(z.astype(jnp.float32)), ssm_norm).astype(u.dtype)
    o_ssm = y @ w_ssm_branch
    merged = jax.nn.sigmoid(g_a) * o_attn + jax.nn.sigmoid(g_m) * o_ssm
    return merged @ w_out


def encoder_trunk(x, norm_mix, w_in, conv_w, conv_b, dt_bias, a_log, d_skip, ssm_norm,
                  w_attn_branch, w_ssm_branch, w_out, norm_mlp, w_mlp_in, w_mlp_out, norm_final):
    h = x
    for layer in range(DEPTH):
        u = rmsnorm(h, norm_mix[layer])
        h = h + mixer_block(u, w_in[layer], conv_w[layer], conv_b[layer], dt_bias[layer], a_log[layer],
                            d_skip[layer], ssm_norm[layer], w_attn_branch[layer], w_ssm_branch[layer], w_out[layer])
        u = rmsnorm(h, norm_mlp[layer])
        h = h + jnp.square(jax.nn.relu(u @ w_mlp_in[layer])) @ w_mlp_out[layer]
    return rmsnorm(h, norm_final)


def setup_inputs(seed: int = 0) -> dict:
    key = jax.random.key(seed)
    ks = jax.random.split(key, 20)
    f32 = jnp.float32

    def nrm(k, shape, scale):
        return jax.random.normal(k, shape, f32) * scale

    dt_init = jnp.exp(jax.random.uniform(ks[6], (DEPTH, 2, SSM_HEADS), f32, math.log(1e-3), math.log(1e-1)))
    return {
        'x_prompt': nrm(ks[0], (BATCH, SEQ, D_MODEL), 1.0),
        'x_sample': nrm(ks[1], (DEC_BATCH, DEC_SEQ, D_MODEL), 1.0),
        'norm_mix': 1.0 + nrm(ks[2], (DEPTH, D_MODEL), 0.02),
        'w_in': nrm(ks[3], (DEPTH, D_MODEL, D_IN_PROJ), D_MODEL ** -0.5),
        'conv_w': nrm(ks[4], (DEPTH, CONV_WIDTH, CONV_DIM), CONV_WIDTH ** -0.5),
        'conv_b': nrm(ks[5], (DEPTH, CONV_DIM), 0.01),
        'dt_bias': dt_init + jnp.log(-jnp.expm1(-dt_init)),
        'a_log': jnp.log(jax.random.uniform(ks[7], (DEPTH, 2, SSM_HEADS), f32, 1.0, 16.0)),
        'd_skip': 1.0 + nrm(ks[8], (DEPTH, SSM_HEADS), 0.1),
        'ssm_norm': 1.0 + nrm(ks[9], (DEPTH, D_INNER), 0.02),
        'w_attn_branch': nrm(ks[10], (DEPTH, ATTN_OUT_WIDTH, D_MODEL), ATTN_OUT_WIDTH ** -0.5),
        'w_ssm_branch': nrm(ks[11], (DEPTH, D_INNER, D_MODEL), D_INNER ** -0.5),
        'w_out': nrm(ks[12], (DEPTH, D_MODEL, D_MODEL), D_MODEL ** -0.5),
        'norm_mlp': 1.0 + nrm(ks[13], (DEPTH, D_MODEL), 0.02),
        'w_mlp_in': nrm(ks[14], (DEPTH, D_MODEL, D_FF), D_MODEL ** -0.5),
        'w_mlp_out': nrm(ks[15], (DEPTH, D_FF, D_MODEL), D_FF ** -0.5),
        'norm_final': 1.0 + nrm(ks[16], (D_MODEL,), 0.02),
    }


def reference(x_prompt, x_sample, norm_mix, w_in, conv_w, conv_b, dt_bias, a_log, d_skip, ssm_norm,
              w_attn_branch, w_ssm_branch, w_out, norm_mlp, w_mlp_in, w_mlp_out, norm_final):
    y_prompt = encoder_trunk(x_prompt, norm_mix, w_in, conv_w, conv_b, dt_bias, a_log, d_skip, ssm_norm,
                             w_attn_branch, w_ssm_branch, w_out, norm_mlp, w_mlp_in, w_mlp_out, norm_final)
    y_sample = encoder_trunk(x_sample, norm_mix, w_in, conv_w, conv_b, dt_bias, a_log, d_skip, ssm_norm,
                             w_attn_branch, w_ssm_branch, w_out, norm_mlp, w_mlp_in, w_mlp_out, norm_final)
    return (y_prompt, y_sample)
```

```python
import functools
import math

import jax
import jax.numpy as jnp
from jax import lax
from jax.experimental import pallas as pl
from jax.experimental.pallas import tpu as pltpu

F32 = jnp.float32
BF16 = jnp.bfloat16

D_MODEL = 1024
HEAD_DIM = 64
DILATIONS = (1, 4, 16)
RADIUS = 64
HEADS_PER_GROUP = 8
GROUP_WIDTH = HEADS_PER_GROUP * HEAD_DIM
QKV_WIDTH = len(DILATIONS) * GROUP_WIDTH
ROPE_THETA = 10000.0
D_INNER = 2048
SSM_GROUPS = 8
HEADS_PER_SSM_GROUP = 4
SSM_HEADS = SSM_GROUPS * HEADS_PER_SSM_GROUP
SSM_GROUP_WIDTH = HEADS_PER_SSM_GROUP * HEAD_DIM
D_STATE = 128
CONV_WIDTH = 5
CONV_DIM = D_INNER + 2 * SSM_GROUPS * D_STATE
CHUNK = 128
D_FF = 4096
RMS_EPS = 1e-6
NEG_INF = -1e30

LANES = 128
CONV_HALO = 16
VMEM_LIMIT_BYTES = 56 * 1024 * 1024

QKV_TOKENS = 512
SSMPROJ_TOKENS = 512
MERGE_TOKENS = 256
MLP_TOKENS = 256
ATTN_QUERY_TILE = 256
ATTN_QUERY_BLOCK = 128
MATMUL_COLS = 512


def _params(semantics):
    return pltpu.CompilerParams(dimension_semantics=semantics, vmem_limit_bytes=VMEM_LIMIT_BYTES)


def _resident(shape):
    zeros = (0,) * len(shape)
    return pl.BlockSpec(shape, lambda *_: zeros, pipeline_mode=pl.Buffered(1))


def _rms_scale(x, w):
    return x * lax.rsqrt(jnp.mean(x * x, axis=-1, keepdims=True) + RMS_EPS) * w


def _dot(a, b):
    return jnp.dot(a, b, preferred_element_type=F32)


def _dot_nt(a, b):
    return lax.dot_general(a, b, (((1,), (1,)), ((), ())), preferred_element_type=F32)


def _dot_tn(a, b):
    return lax.dot_general(a, b, (((0,), (0,)), ((), ())), preferred_element_type=F32)


def _qkv_kernel(x_ref, nw_ref, w_ref, cos_ref, sin_ref, q_ref, k_ref, v_ref):
    u = _rms_scale(x_ref[...], nw_ref[...]).astype(BF16)
    cos = cos_ref[...]
    sin = sin_ref[...]
    lane = lax.broadcasted_iota(jnp.int32, cos.shape, 1)
    first_half = (lane % HEAD_DIM) < (HEAD_DIM // 2)

    def rope(p):
        partner = jnp.where(first_half, pltpu.roll(p, LANES - HEAD_DIM // 2, axis=1),
                            pltpu.roll(p, HEAD_DIM // 2, axis=1))
        return p * cos + partner * sin

    for seg, (o_ref, rotary, scale) in enumerate(((q_ref, True, 1.0 / math.sqrt(HEAD_DIM)),
                                                  (k_ref, True, 1.0), (v_ref, False, 1.0))):
        for c in range(QKV_WIDTH // MATMUL_COLS):
            col = seg * QKV_WIDTH + c * MATMUL_COLS
            acc = _dot(u, w_ref[:, col:col + MATMUL_COLS])
            for t in range(MATMUL_COLS // LANES):
                piece = acc[:, t * LANES:(t + 1) * LANES]
                if rotary:
                    piece = rope(piece) * scale
                o_ref[:, c * MATMUL_COLS + t * LANES:c * MATMUL_COLS + (t + 1) * LANES] = piece.astype(BF16)


def _qkv_call(x2, nw, w_qkv, cos, sin, seq_len):
    tokens = x2.shape[0]
    tm = QKV_TOKENS
    pos_blocks = seq_len // tm
    out = jax.ShapeDtypeStruct((tokens, QKV_WIDTH), BF16)
    tok = lambda i: (i, 0)
    pos = lambda i: (i % pos_blocks, 0)
    return pl.pallas_call(
        _qkv_kernel,
        out_shape=(out, out, out),
        grid=(tokens // tm,),
        in_specs=[pl.BlockSpec((tm, D_MODEL), tok), _resident((1, D_MODEL)),
                  _resident((D_MODEL, 3 * QKV_WIDTH)),
                  pl.BlockSpec((tm, LANES), pos), pl.BlockSpec((tm, LANES), pos)],
        out_specs=(pl.BlockSpec((tm, QKV_WIDTH), tok),) * 3,
        compiler_params=_params(("parallel",)),
        name="qkv_proj",
    )(x2, nw, w_qkv, cos, sin)


def _softplus(x):
    return jnp.maximum(x, 0.0) + jnp.log1p(jnp.exp(-jnp.abs(x)))


def _ssmproj_kernel(x_ref, nw_ref, wz_ref, wx_ref, wdt_ref, dtb_ref, z_ref, xbc_ref, dt_ref):
    u32 = _rms_scale(x_ref[...], nw_ref[...])
    u = u32.astype(BF16)
    for c in range(D_INNER // MATMUL_COLS):
        sl = slice(c * MATMUL_COLS, (c + 1) * MATMUL_COLS)
        z_ref[:, sl] = _dot(u, wz_ref[:, sl]).astype(BF16)
    for c in range(CONV_DIM // MATMUL_COLS):
        sl = slice(c * MATMUL_COLS, (c + 1) * MATMUL_COLS)
        xbc_ref[:, sl] = _dot(u, wx_ref[:, sl]).astype(BF16)
    dt_t = lax.dot_general(wdt_ref[...], u32, (((1,), (1,)), ((), ())),
                           precision=lax.Precision.HIGHEST, preferred_element_type=F32)
    dt_t = _softplus(dt_t + dtb_ref[...])
    rows = 2 * HEADS_PER_SSM_GROUP
    for g in range(SSM_GROUPS):
        for c in range(dt_ref.shape[1]):
            dt_ref[g, c] = dt_t[g * rows:(g + 1) * rows, c * CHUNK:(c + 1) * CHUNK]


def _ssmproj_call(x2, nw, w_z, w_xbc, w_dt_t, dt_bias, batch, seq_len):
    tokens = x2.shape[0]
    tm = SSMPROJ_TOKENS
    per_seq = seq_len // tm
    rows = 2 * HEADS_PER_SSM_GROUP
    tok = lambda i: (i, 0)
    return pl.pallas_call(
        _ssmproj_kernel,
        out_shape=(jax.ShapeDtypeStruct((tokens, D_INNER), BF16),
                   jax.ShapeDtypeStruct((tokens, CONV_DIM), BF16),
                   jax.ShapeDtypeStruct((SSM_GROUPS, batch, seq_len // CHUNK, rows, CHUNK), F32)),
        grid=(tokens // tm,),
        in_specs=[pl.BlockSpec((tm, D_MODEL), tok), _resident((1, D_MODEL)),
                  _resident((D_MODEL, D_INNER)), _resident((D_MODEL, CONV_DIM)),
                  _resident((2 * SSM_HEADS, D_MODEL)), _resident((2 * SSM_HEADS, 1))],
        out_specs=(pl.BlockSpec((tm, D_INNER), tok), pl.BlockSpec((tm, CONV_DIM), tok),
                   pl.BlockSpec((SSM_GROUPS, None, tm // CHUNK, rows, CHUNK),
                                lambda i: (0, i // per_seq, i % per_seq, 0, 0))),
        compiler_params=_params(("parallel",)),
        name="ssm_proj",
    )(x2, nw, w_z, w_xbc, w_dt_t, dt_bias)


def _attn_kernel(q_ref, kp_ref, km_ref, kn_ref, vp_ref, vm_ref, vn_ref, o_ref, lse_ref, *, tq, sub_len):
    j = pl.program_id(2)
    q = q_ref[...]
    k = jnp.concatenate([kp_ref[...], km_ref[...], kn_ref[...]], axis=0)
    v = jnp.concatenate([vp_ref[...], vm_ref[...], vn_ref[...]], axis=0)
    qb = ATTN_QUERY_BLOCK
    kb = qb + 2 * RADIUS
    row = lax.broadcasted_iota(jnp.int32, (qb, kb), 0)
    col = lax.broadcasted_iota(jnp.int32, (qb, kb), 1)
    band = (col >= row) & (col <= row + 2 * RADIUS)
    head_lane = lax.broadcasted_iota(jnp.int32, (qb, LANES), 1)
    for sb in range(tq // qb):
        key_pos = j * tq + (sb * qb - RADIUS) + col
        valid = band & (key_pos >= 0) & (key_pos < sub_len)
        qs = q[sb * qb:(sb + 1) * qb]
        ks = k[sb * qb:sb * qb + kb]
        vs = v[sb * qb:sb * qb + kb]
        lse_tile = jnp.zeros((qb, LANES), F32)
        for h in range(HEADS_PER_GROUP):
            sl = slice(h * HEAD_DIM, (h + 1) * HEAD_DIM)
            s = jnp.where(valid, _dot_nt(qs[:, sl], ks[:, sl]), NEG_INF)
            m = jnp.max(s, axis=-1, keepdims=True)
            p = jnp.exp(s - m)
            l = jnp.sum(p, axis=-1, keepdims=True)
            o = _dot(p.astype(BF16), vs[:, sl])
            o_ref[sb * qb:(sb + 1) * qb, sl] = (o / l).astype(BF16)
            lse_tile = jnp.where(head_lane == h, m + jnp.log(l), lse_tile)
        lse_ref[sb * qb:(sb + 1) * qb, :] = lse_tile


def _attn_call(q, k, v, group, dilation, batch, seq_len):
    sub_len = seq_len // dilation
    tq = min(ATTN_QUERY_TILE, sub_len)
    halo_per_tile = tq // RADIUS
    last_halo = sub_len // RADIUS - 1
    cols = QKV_WIDTH // GROUP_WIDTH
    view = lambda a, w: a.reshape(batch, sub_len, dilation * w)
    main = lambda b, r, j: (b, j, r * cols + group)
    prev = lambda b, r, j: (b, jnp.maximum(j * halo_per_tile - 1, 0), r * cols + group)
    nxt = lambda b, r, j: (b, jnp.minimum((j + 1) * halo_per_tile, last_halo), r * cols + group)
    out_map = lambda b, r, j: (b, j, r)
    tile = lambda m: pl.BlockSpec((None, tq, GROUP_WIDTH), m)
    halo = lambda m: pl.BlockSpec((None, RADIUS, GROUP_WIDTH), m)
    qv, kv, vv = view(q, QKV_WIDTH), view(k, QKV_WIDTH), view(v, QKV_WIDTH)
    out, lse = pl.pallas_call(
        functools.partial(_attn_kernel, tq=tq, sub_len=sub_len),
        out_shape=(jax.ShapeDtypeStruct((batch, sub_len, dilation * GROUP_WIDTH), BF16),
                   jax.ShapeDtypeStruct((batch, sub_len, dilation * LANES), F32)),
        grid=(batch, dilation, sub_len // tq),
        in_specs=[tile(main), halo(prev), tile(main), halo(nxt), halo(prev), tile(main), halo(nxt)],
        out_specs=(pl.BlockSpec((None, tq, GROUP_WIDTH), out_map), pl.BlockSpec((None, tq, LANES), out_map)),
        compiler_params=_params(("parallel", "parallel", "parallel")),
        name=f"band_attn_d{dilation}",
    )(qv, kv, kv, kv, vv, vv, vv)
    return out.reshape(batch * seq_len, GROUP_WIDTH), lse.reshape(batch * seq_len, LANES)


def _ssd_kernel(xp_ref, bp_ref, cp_ref, dt_ref, cwx_ref, cwb_ref, cwc_ref, cbx_ref, cbb_ref, cbc_ref,
                alog_ref, dsk_ref, y_ref, xs_s, bm_s, cm_s, yacc_s, hf_s, hb_s, *, seq_len):
    L = CHUNK
    nc = seq_len // L
    heads = HEADS_PER_SSM_GROUP

    def conv_chunk(c, carry):
        r0 = pl.multiple_of(c * L, L)
        lo = pl.multiple_of(jnp.maximum(r0 - CONV_HALO, 0), CONV_HALO)
        hi = pl.multiple_of(jnp.minimum(r0 + L, seq_len - CONV_HALO), CONV_HALO)
        for src, w_ref, b_ref, dst in ((xp_ref, cwx_ref, cbx_ref, xs_s), (bp_ref, cwb_ref, cbb_ref, bm_s),
                                       (cp_ref, cwc_ref, cbc_ref, cm_s)):
            cur = src[pl.ds(r0, L), :].astype(F32)
            before = jnp.where(c > 0, src[pl.ds(lo, CONV_HALO), :].astype(F32), 0.0)
            after = jnp.where(c < nc - 1, src[pl.ds(hi, CONV_HALO), :].astype(F32), 0.0)
            ext = jnp.concatenate([before, cur, after], axis=0)
            w = w_ref[...]
            acc = b_ref[...] + ext[CONV_HALO - 2:CONV_HALO - 2 + L] * w[0:1]
            for tap in range(1, CONV_WIDTH):
                acc = acc + ext[CONV_HALO - 2 + tap:CONV_HALO - 2 + tap + L] * w[tap:tap + 1]
            dst[pl.ds(r0, L), :] = (acc * jax.nn.sigmoid(acc)).astype(BF16)
        return carry

    lax.fori_loop(0, nc, conv_chunk, 0)

    hf_s[...] = jnp.zeros_like(hf_s)
    hb_s[...] = jnp.zeros_like(hb_s)
    a_all = -jnp.exp(alog_ref[...])
    dskip = dsk_ref[...]
    lane = lax.broadcasted_iota(jnp.int32, (2 * heads, L), 1)
    row = lax.broadcasted_iota(jnp.int32, (L, L), 0)
    col = lax.broadcasted_iota(jnp.int32, (L, L), 1)
    causal = (row >= col, row <= col)
    pad_rows = jnp.zeros((L - 4 * heads, L), F32)

    def chunk(c, backward, h_s, first_touch):
        r0 = pl.multiple_of(c * L, L)
        xs = xs_s[pl.ds(r0, L), :].astype(F32)
        bm = bm_s[pl.ds(r0, L), :]
        cm = cm_s[pl.ds(r0, L), :]
        dt8 = dt_ref[c]
        dta = dt8 * a_all
        cs = dta
        shift = 1
        while shift < L:
            cs = cs + jnp.where(lane >= shift, pltpu.roll(cs, shift, axis=1), 0.0)
            shift *= 2
        total = cs[:, L - 1:L]
        acs = (total - cs + dta) if backward else cs
        cols = jnp.transpose(jnp.concatenate([acs, dt8, pad_rows], axis=0))
        cb = _dot_nt(cm, bm)
        for e in range(heads):
            hr = e + heads if backward else e
            sl = slice(e * HEAD_DIM, (e + 1) * HEAD_DIM)
            col_acs = cols[:, hr:hr + 1]
            col_dt = cols[:, 2 * heads + hr:2 * heads + hr + 1]
            tot = total[hr:hr + 1, :]
            seg = col_acs - acs[hr:hr + 1, :]
            w = cb * jnp.exp(jnp.where(causal[backward], seg, -jnp.inf))
            xe = xs[:, sl]
            xdt = xe * col_dt
            y_e = _dot(w.astype(BF16), xdt.astype(BF16))
            h_prev = h_s[e]
            y_e = y_e + _dot(cm, h_prev.astype(BF16)) * jnp.exp(col_acs)
            x_dec = (xdt * jnp.exp(tot - col_acs)).astype(BF16)
            h_s[e] = h_prev * jnp.exp(tot) + _dot_tn(bm, x_dec)
            if not backward:
                y_e = y_e + xe * dskip[:, sl]
            if first_touch:
                yacc_s[pl.ds(r0, L), sl] = y_e
            else:
                y_ref[pl.ds(r0, L), sl] = (yacc_s[pl.ds(r0, L), sl] + y_e).astype(BF16)

    def sweep(first_touch):
        def body(i, carry):
            chunk(i, False, hf_s, first_touch)
            chunk(nc - 1 - i, True, hb_s, first_touch)
            return carry
        return body

    lax.fori_loop(0, nc // 2, sweep(True), 0)
    lax.fori_loop(nc // 2, nc, sweep(False), 0)


def _ssd_call(xbc, dt, conv_w, conv_b, a_log, d_skip, batch, seq_len):
    assert (seq_len // CHUNK) % 2 == 0
    xg = D_INNER // SSM_GROUP_WIDTH
    b_off = D_INNER // D_STATE
    c_off = b_off + SSM_GROUPS
    seq_block = lambda w, m: pl.BlockSpec((None, seq_len, w), m)
    rows = 2 * HEADS_PER_SSM_GROUP
    return pl.pallas_call(
        functools.partial(_ssd_kernel, seq_len=seq_len),
        out_shape=jax.ShapeDtypeStruct((batch, seq_len, D_INNER), BF16),
        grid=(batch, SSM_GROUPS),
        in_specs=[seq_block(SSM_GROUP_WIDTH, lambda b, g: (b, 0, g)),
                  seq_block(D_STATE, lambda b, g: (b, 0, b_off + g)),
                  seq_block(D_STATE, lambda b, g: (b, 0, c_off + g)),
                  pl.BlockSpec((None, None, seq_len // CHUNK, rows, CHUNK), lambda b, g: (g, b, 0, 0, 0)),
                  pl.BlockSpec((CONV_WIDTH, SSM_GROUP_WIDTH), lambda b, g: (0, g)),
                  pl.BlockSpec((CONV_WIDTH, D_STATE), lambda b, g: (0, b_off + g)),
                  pl.BlockSpec((CONV_WIDTH, D_STATE), lambda b, g: (0, c_off + g)),
                  pl.BlockSpec((1, SSM_GROUP_WIDTH), lambda b, g: (0, g)),
                  pl.BlockSpec((1, D_STATE), lambda b, g: (0, b_off + g)),
                  pl.BlockSpec((1, D_STATE), lambda b, g: (0, c_off + g)),
                  pl.BlockSpec((None, rows, 1), lambda b, g: (g, 0, 0)),
                  pl.BlockSpec((None, 1, SSM_GROUP_WIDTH), lambda b, g: (g, 0, 0))],
        out_specs=seq_block(SSM_GROUP_WIDTH, lambda b, g: (b, 0, g)),
        scratch_shapes=[pltpu.VMEM((seq_len, SSM_GROUP_WIDTH), BF16), pltpu.VMEM((seq_len, D_STATE), BF16),
                        pltpu.VMEM((seq_len, D_STATE), BF16), pltpu.VMEM((seq_len, SSM_GROUP_WIDTH), F32),
                        pltpu.VMEM((HEADS_PER_SSM_GROUP, D_STATE, HEAD_DIM), F32),
                        pltpu.VMEM((HEADS_PER_SSM_GROUP, D_STATE, HEAD_DIM), F32)],
        compiler_params=_params(("parallel", "parallel")),
        name="conv_ssd",
    )(xbc, xbc, xbc, dt, conv_w, conv_w, conv_w, conv_b, conv_b, conv_b, a_log, d_skip)


def _merge_kernel(x_ref, o0_ref, o1_ref, o2_ref, l0_ref, l1_ref, l2_ref, y_ref, z_ref, nw_ref, sn_ref,
                  wga_ref, wgm_ref, wab_ref, wsb_ref, wo_ref, ex_ref, h_ref):
    x = x_ref[...]
    u = _rms_scale(x, nw_ref[...]).astype(BF16)
    lses = (l0_ref[...], l1_ref[...], l2_ref[...])
    top = jnp.maximum(jnp.maximum(lses[0], lses[1]), lses[2])
    es = [jnp.exp(l - top) for l in lses]
    inv = 1.0 / (es[0] + es[1] + es[2])
    expand = ex_ref[...]
    mix = jnp.zeros((x.shape[0], GROUP_WIDTH), F32)
    for e, o_ref in zip(es, (o0_ref, o1_ref, o2_ref)):
        alpha = e * inv
        hi = alpha.astype(BF16)
        lo = (alpha - hi.astype(F32)).astype(BF16)
        mix = mix + (_dot(hi, expand) + _dot(lo, expand)) * o_ref[...].astype(F32)
    o_attn = _dot(mix.astype(BF16), wab_ref[...])
    z = z_ref[...].astype(F32)
    yz = y_ref[...].astype(F32) * (z * jax.nn.sigmoid(z))
    o_ssm = _dot(_rms_scale(yz, sn_ref[...]).astype(BF16), wsb_ref[...])
    merged = jax.nn.sigmoid(_dot(u, wga_ref[...])) * o_attn + jax.nn.sigmoid(_dot(u, wgm_ref[...])) * o_ssm
    h_ref[...] = x + _dot(merged.astype(BF16), wo_ref[...])


def _merge_call(x2, outs, lses, y, z, nw, ssm_norm, w_ga, w_gm, w_ab, w_sb, w_out, expand):
    tokens = x2.shape[0]
    tm = MERGE_TOKENS
    tok = lambda w: pl.BlockSpec((tm, w), lambda i: (i, 0))
    return pl.pallas_call(
        _merge_kernel,
        out_shape=jax.ShapeDtypeStruct((tokens, D_MODEL), F32),
        grid=(tokens // tm,),
        in_specs=[tok(D_MODEL), tok(GROUP_WIDTH), tok(GROUP_WIDTH), tok(GROUP_WIDTH),
                  tok(LANES), tok(LANES), tok(LANES), tok(D_INNER), tok(D_INNER),
                  _resident((1, D_MODEL)), _resident((1, D_INNER)),
                  _resident((D_MODEL, D_MODEL)), _resident((D_MODEL, D_MODEL)),
                  _resident((GROUP_WIDTH, D_MODEL)), _resident((D_INNER, D_MODEL)),
                  _resident((D_MODEL, D_MODEL)), _resident((LANES, GROUP_WIDTH))],
        out_specs=tok(D_MODEL),
        compiler_params=_params(("parallel",)),
        name="merge_out",
    )(x2, *outs, *lses, y, z, nw, ssm_norm, w_ga, w_gm, w_ab, w_sb, w_out, expand)


def _mlp_kernel(h_ref, nw_ref, w1_ref, w2_ref, nf_ref, o_ref):
    h = h_ref[...]
    u = _rms_scale(h, nw_ref[...]).astype(BF16)
    acc = h
    for c in range(D_FF // MATMUL_COLS):
        sl = slice(c * MATMUL_COLS, (c + 1) * MATMUL_COLS)
        a = jnp.maximum(_dot(u, w1_ref[:, sl]), 0.0)
        acc = acc + _dot((a * a).astype(BF16), w2_ref[sl, :])
    o_ref[...] = _rms_scale(acc, nf_ref[...])


def _mlp_call(h, nw, w1, w2, nf):
    tokens = h.shape[0]
    tm = MLP_TOKENS
    tok = pl.BlockSpec((tm, D_MODEL), lambda i: (i, 0))
    return pl.pallas_call(
        _mlp_kernel,
        out_shape=jax.ShapeDtypeStruct((tokens, D_MODEL), F32),
        grid=(tokens // tm,),
        in_specs=[tok, _resident((1, D_MODEL)), _resident((D_MODEL, D_FF)), _resident((D_FF, D_MODEL)),
                  _resident((1, D_MODEL))],
        out_specs=tok,
        compiler_params=_params(("parallel",)),
        name="mlp_norm",
    )(h, nw, w1, w2, nf)


def _rope_tables(seq_len):
    half = HEAD_DIM // 2
    inv = ROPE_THETA ** (-(jnp.arange(half, dtype=F32) * 2.0 / HEAD_DIM))
    ang = jnp.arange(seq_len, dtype=F32)[:, None] * inv[None, :]
    cos, sin = jnp.cos(ang), jnp.sin(ang)
    reps = LANES // HEAD_DIM
    return (jnp.tile(jnp.concatenate([cos, cos], axis=-1), (1, reps)),
            jnp.tile(jnp.concatenate([-sin, sin], axis=-1), (1, reps)))


def _trunk(x, p):
    batch, seq_len, _ = x.shape
    x2 = x.reshape(batch * seq_len, D_MODEL)
    cos, sin = _rope_tables(seq_len)
    q, k, v = _qkv_call(x2, p["norm_mix"], p["w_qkv"], cos, sin, seq_len)
    z, xbc, dt = _ssmproj_call(x2, p["norm_mix"], p["w_z"], p["w_xbc"], p["w_dt_t"], p["dt_bias"], batch, seq_len)
    outs, lses = [], []
    for group, dilation in enumerate(DILATIONS):
        o, lse = _attn_call(q, k, v, group, dilation, batch, seq_len)
        outs.append(o)
        lses.append(lse)
    y = _ssd_call(xbc.reshape(batch, seq_len, CONV_DIM), dt, p["conv_w"], p["conv_b"], p["a_log"], p["d_skip"],
                  batch, seq_len)
    h = _merge_call(x2, outs, lses, y.reshape(batch * seq_len, D_INNER), z, p["norm_mix"], p["ssm_norm"],
                    p["w_ga"], p["w_gm"], p["w_ab"], p["w_sb"], p["w_out"], p["expand"])
    out = _mlp_call(h, p["norm_mlp"], p["w_mlp_in"], p["w_mlp_out"], p["norm_final"])
    return out.reshape(batch, seq_len, D_MODEL)


def _prepare(norm_mix, w_in, conv_w, conv_b, dt_bias, a_log, d_skip, ssm_norm, w_attn_branch, w_ssm_branch,
             w_out, norm_mlp, w_mlp_in, w_mlp_out, norm_final):
    w = w_in[0]
    bounds = {}
    start = 0
    for name, width in (("qkv", 3 * QKV_WIDTH), ("z", D_INNER), ("xbc", CONV_DIM), ("dt", 2 * SSM_HEADS),
                        ("ga", D_MODEL), ("gm", D_MODEL)):
        bounds[name] = (start, start + width)
        start += width
    cut = lambda name: w[:, bounds[name][0]:bounds[name][1]]
    perm = jnp.arange(2 * SSM_HEADS).reshape(2, SSM_GROUPS, HEADS_PER_SSM_GROUP).transpose(1, 0, 2).reshape(-1)
    heads_lanes = jnp.arange(LANES)[:, None] == (jnp.arange(GROUP_WIDTH)[None, :] // HEAD_DIM)
    return {
        "norm_mix": norm_mix[0][None, :],
        "w_qkv": cut("qkv").astype(BF16),
        "w_z": cut("z").astype(BF16),
        "w_xbc": cut("xbc").astype(BF16),
        "w_dt_t": cut("dt")[:, perm].T,
        "dt_bias": dt_bias[0].reshape(-1)[perm][:, None],
        "a_log": a_log[0].reshape(-1)[perm].reshape(SSM_GROUPS, 2 * HEADS_PER_SSM_GROUP, 1),
        "d_skip": jnp.repeat(d_skip[0].reshape(SSM_GROUPS, HEADS_PER_SSM_GROUP), HEAD_DIM, axis=1)[:, None, :],
        "conv_w": conv_w[0],
        "conv_b": conv_b[0][None, :],
        "ssm_norm": ssm_norm[0][None, :],
        "w_ga": cut("ga").astype(BF16),
        "w_gm": cut("gm").astype(BF16),
        "w_ab": w_attn_branch[0].astype(BF16),
        "w_sb": w_ssm_branch[0].astype(BF16),
        "w_out": w_out[0].astype(BF16),
        "expand": heads_lanes.astype(BF16),
        "norm_mlp": norm_mlp[0][None, :],
        "w_mlp_in": w_mlp_in[0].astype(BF16),
        "w_mlp_out": w_mlp_out[0].astype(BF16),
        "norm_final": norm_final[None, :],
    }


def kernel(x_prompt, x_sample, norm_mix, w_in, conv_w, conv_b, dt_bias, a_log, d_skip, ssm_norm, w_attn_branch,
           w_ssm_branch, w_out, norm_mlp, w_mlp_in, w_mlp_out, norm_final):
    p = _prepare(norm_mix, w_in, conv_w, conv_b, dt_bias, a_log, d_skip, ssm_norm, w_attn_branch, w_ssm_branch,
                 w_out, norm_mlp, w_mlp_in, w_mlp_out, norm_final)
    return _trunk(x_prompt, p), _trunk(x_sample, p)
```

```python
import functools
import math

import jax
import jax.numpy as jnp
from jax import lax
from jax.experimental import pallas as pl
from jax.experimental.pallas import tpu as pltpu

F32 = jnp.float32
BF16 = jnp.bfloat16

D_MODEL = 1024
HEAD_DIM = 64
DILATIONS = (1, 4, 16)
RADIUS = 64
HEADS_PER_GROUP = 8
GROUP_WIDTH = HEADS_PER_GROUP * HEAD_DIM
QKV_WIDTH = len(DILATIONS) * GROUP_WIDTH
ROPE_THETA = 10000.0
D_INNER = 2048
SSM_GROUPS = 8
HEADS_PER_SSM_GROUP = 4
SSM_HEADS = SSM_GROUPS * HEADS_PER_SSM_GROUP
SSM_GROUP_WIDTH = HEADS_PER_SSM_GROUP * HEAD_DIM
D_STATE = 128
CONV_WIDTH = 5
CONV_DIM = D_INNER + 2 * SSM_GROUPS * D_STATE
CHUNK = 128
D_FF = 4096
RMS_EPS = 1e-6
NEG_INF = -1e30

LANES = 128
CONV_HALO = 16
VMEM_LIMIT_BYTES = 56 * 1024 * 1024

QKV_TOKENS = 512
SSMPROJ_TOKENS = 512
MERGE_TOKENS = 256
MLP_TOKENS = 256
ATTN_QUERY_TILE = {1: 512, 4: 256, 16: 128}
ATTN_QUERY_BLOCK = 128
MATMUL_COLS = 512


def _params(semantics):
    return pltpu.CompilerParams(dimension_semantics=semantics, vmem_limit_bytes=VMEM_LIMIT_BYTES)


def _resident(shape):
    zeros = (0,) * len(shape)
    return pl.BlockSpec(shape, lambda *_: zeros, pipeline_mode=pl.Buffered(1))


def _rms_scale(x, w):
    return x * lax.rsqrt(jnp.mean(x * x, axis=-1, keepdims=True) + RMS_EPS) * w


def _dot(a, b):
    return jnp.dot(a, b, preferred_element_type=F32)


def _dot_nt(a, b):
    return lax.dot_general(a, b, (((1,), (1,)), ((), ())), preferred_element_type=F32)


def _qkv_kernel(x_ref, nw_ref, w_ref, cos_ref, sin_ref, *refs):
    out_refs, stage = refs[:-1], refs[-1]
    tm = x_ref.shape[0]
    u = _rms_scale(x_ref[...], nw_ref[...]).astype(BF16)
    cos = cos_ref[...]
    sin = sin_ref[...]
    lane = lax.broadcasted_iota(jnp.int32, cos.shape, 1)
    first_half = (lane % HEAD_DIM) < (HEAD_DIM // 2)

    def rope(p):
        partner = jnp.where(first_half, pltpu.roll(p, LANES - HEAD_DIM // 2, axis=1),
                            pltpu.roll(p, HEAD_DIM // 2, axis=1))
        return p * cos + partner * sin

    for seg, (rotary, scale) in enumerate(((True, 1.0 / math.sqrt(HEAD_DIM)), (True, 1.0), (False, 1.0))):
        for group, dilation in enumerate(DILATIONS):
            o_ref = out_refs[seg * len(DILATIONS) + group]
            col = seg * QKV_WIDTH + group * GROUP_WIDTH
            acc = _dot(u, w_ref[:, col:col + GROUP_WIDTH])
            for t in range(GROUP_WIDTH // LANES):
                piece = acc[:, t * LANES:(t + 1) * LANES]
                if rotary:
                    piece = rope(piece) * scale
                if dilation == 1:
                    o_ref[:, t * LANES:(t + 1) * LANES] = piece.astype(BF16)
                else:
                    stage[t] = piece
                    for r in range(dilation):
                        o_ref[:, r * GROUP_WIDTH + t * LANES:r * GROUP_WIDTH + (t + 1) * LANES] = (
                            stage[t, pl.ds(r, tm // dilation, stride=dilation), :].astype(BF16))


def _qkv_call(x2, nw, w_qkv, cos, sin, seq_len):
    tokens = x2.shape[0]
    tm = QKV_TOKENS
    pos_blocks = seq_len // tm
    tok = lambda i: (i, 0)
    pos = lambda i: (i % pos_blocks, 0)
    shapes, specs = [], []
    for _ in range(3):
        for d in DILATIONS:
            shapes.append(jax.ShapeDtypeStruct((tokens // d, d * GROUP_WIDTH), BF16))
            specs.append(pl.BlockSpec((tm // d, d * GROUP_WIDTH), tok))
    return pl.pallas_call(
        _qkv_kernel,
        out_shape=tuple(shapes),
        grid=(tokens // tm,),
        in_specs=[pl.BlockSpec((tm, D_MODEL), tok), _resident((1, D_MODEL)),
                  _resident((D_MODEL, 3 * QKV_WIDTH)),
                  pl.BlockSpec((tm, LANES), pos), pl.BlockSpec((tm, LANES), pos)],
        out_specs=tuple(specs),
        scratch_shapes=[pltpu.VMEM((GROUP_WIDTH // LANES, tm, LANES), F32)],
        compiler_params=_params(("parallel",)),
        name="qkv_proj",
    )(x2, nw, w_qkv, cos, sin)


def _softplus(x):
    return jnp.maximum(x, 0.0) + jnp.log1p(jnp.exp(-jnp.abs(x)))


def _ssmproj_kernel(x_ref, nw_ref, wz_ref, wx_ref, wdt_ref, dtb_ref, z_ref, xbc_ref, dt_ref):
    u32 = _rms_scale(x_ref[...], nw_ref[...])
    u = u32.astype(BF16)
    for c in range(D_INNER // MATMUL_COLS):
        sl = slice(c * MATMUL_COLS, (c + 1) * MATMUL_COLS)
        z_ref[:, sl] = _dot(u, wz_ref[:, sl]).astype(BF16)
    for c in range(CONV_DIM // MATMUL_COLS):
        sl = slice(c * MATMUL_COLS, (c + 1) * MATMUL_COLS)
        xbc_ref[:, sl] = _dot(u, wx_ref[:, sl]).astype(BF16)
    dt_t = lax.dot_general(wdt_ref[...], u32, (((1,), (1,)), ((), ())),
                           precision=lax.Precision.HIGHEST, preferred_element_type=F32)
    dt_t = _softplus(dt_t + dtb_ref[...])
    rows = 2 * HEADS_PER_SSM_GROUP
    for g in range(SSM_GROUPS):
        for c in range(dt_ref.shape[1] // rows):
            dt_ref[g, c * rows:(c + 1) * rows, :] = dt_t[g * rows:(g + 1) * rows, c * CHUNK:(c + 1) * CHUNK]


def _ssmproj_call(x2, nw, w_z, w_xbc, w_dt_t, dt_bias, batch, seq_len):
    tokens = x2.shape[0]
    tm = SSMPROJ_TOKENS
    per_seq = seq_len // tm
    rows = 2 * HEADS_PER_SSM_GROUP
    tok = lambda i: (i, 0)
    return pl.pallas_call(
        _ssmproj_kernel,
        out_shape=(jax.ShapeDtypeStruct((tokens, D_INNER), BF16),
                   jax.ShapeDtypeStruct((tokens, CONV_DIM), BF16),
                   jax.ShapeDtypeStruct((SSM_GROUPS, batch, seq_len // CHUNK * rows, CHUNK), F32)),
        grid=(tokens // tm,),
        in_specs=[pl.BlockSpec((tm, D_MODEL), tok), _resident((1, D_MODEL)),
                  _resident((D_MODEL, D_INNER)), _resident((D_MODEL, CONV_DIM)),
                  _resident((2 * SSM_HEADS, D_MODEL)), _resident((2 * SSM_HEADS, 1))],
        out_specs=(pl.BlockSpec((tm, D_INNER), tok), pl.BlockSpec((tm, CONV_DIM), tok),
                   pl.BlockSpec((SSM_GROUPS, None, tm // CHUNK * rows, CHUNK),
                                lambda i: (0, i // per_seq, i % per_seq, 0))),
        compiler_params=_params(("parallel",)),
        name="ssm_proj",
    )(x2, nw, w_z, w_xbc, w_dt_t, dt_bias)


def _attn_kernel(q_ref, kp_ref, km_ref, kn_ref, vp_ref, vm_ref, vn_ref, o_ref, lse_ref, o_s, *, tq, sub_len,
                 dilation):
    j = pl.program_id(1)
    r = pl.program_id(2)
    q = q_ref[...]
    k = jnp.concatenate([kp_ref[...], km_ref[...], kn_ref[...]], axis=0)
    v = jnp.concatenate([vp_ref[...], vm_ref[...], vn_ref[...]], axis=0)
    qb = min(ATTN_QUERY_BLOCK, tq)
    kb = qb + 2 * RADIUS
    row = lax.broadcasted_iota(jnp.int32, (qb, kb), 0)
    col = lax.broadcasted_iota(jnp.int32, (qb, kb), 1)
    band = (col >= row) & (col <= row + 2 * RADIUS)
    head_lane = lax.broadcasted_iota(jnp.int32, (qb, LANES), 1)
    low_q = lax.broadcasted_iota(jnp.int32, (qb, LANES), 1) < HEAD_DIM
    low_k = lax.broadcasted_iota(jnp.int32, (kb, LANES), 1) < HEAD_DIM
    for sb in range(tq // qb):
        key_pos = j * tq + (sb * qb - RADIUS) + col
        valid = band & (key_pos >= 0) & (key_pos < sub_len)
        rows = pl.ds(r + sb * qb * dilation, qb, stride=dilation) if dilation > 1 else pl.ds(sb * qb, qb)
        lse_tile = jnp.zeros((qb, LANES), F32)
        for pair in range(HEADS_PER_GROUP // 2):
            sl = slice(pair * LANES, (pair + 1) * LANES)
            qp = q[sb * qb:(sb + 1) * qb, sl]
            kp = k[sb * qb:sb * qb + kb, sl]
            vp = v[sb * qb:sb * qb + kb, sl]
            o_pair = jnp.zeros((qb, LANES), F32)
            for half in range(2):
                keep_q = low_q if half == 0 else ~low_q
                keep_k = low_k if half == 0 else ~low_k
                s = jnp.where(valid, _dot_nt(jnp.where(keep_q, qp, 0), kp), NEG_INF)
                m = jnp.max(s, axis=-1, keepdims=True)
                p = jnp.exp(s - m)
                l = jnp.sum(p, axis=-1, keepdims=True)
                o_pair = o_pair + _dot(p.astype(BF16), jnp.where(keep_k, vp, 0)) / l
                lse_tile = jnp.where(head_lane == 2 * pair + half, m + jnp.log(l), lse_tile)
            o_s[pair, rows, :] = o_pair
        lse_ref[rows, :] = lse_tile

    @pl.when(r == dilation - 1)
    def _():
        for pair in range(HEADS_PER_GROUP // 2):
            o_ref[:, pair * LANES:(pair + 1) * LANES] = o_s[pair].astype(BF16)


def _attn_call(q, k, v, dilation, batch, seq_len):
    sub_len = seq_len // dilation
    tq = min(ATTN_QUERY_TILE[dilation], sub_len)
    halo_per_tile = tq // RADIUS
    last_halo = sub_len // RADIUS - 1
    view = lambda a: a.reshape(batch, sub_len, dilation * GROUP_WIDTH)
    main = lambda b, j, r: (b, j, r)
    prev = lambda b, j, r: (b, jnp.maximum(j * halo_per_tile - 1, 0), r)
    nxt = lambda b, j, r: (b, jnp.minimum((j + 1) * halo_per_tile, last_halo), r)
    out_map = lambda b, j, r: (b, j, 0)
    tile = lambda m: pl.BlockSpec((None, tq, GROUP_WIDTH), m)
    halo = lambda m: pl.BlockSpec((None, RADIUS, GROUP_WIDTH), m)
    qv, kv, vv = view(q), view(k), view(v)
    out, lse = pl.pallas_call(
        functools.partial(_attn_kernel, tq=tq, sub_len=sub_len, dilation=dilation),
        out_shape=(jax.ShapeDtypeStruct((batch, seq_len, GROUP_WIDTH), BF16),
                   jax.ShapeDtypeStruct((batch, seq_len, LANES), F32)),
        grid=(batch, sub_len // tq, dilation),
        in_specs=[tile(main), halo(prev), tile(main), halo(nxt), halo(prev), tile(main), halo(nxt)],
        out_specs=(pl.BlockSpec((None, tq * dilation, GROUP_WIDTH), out_map),
                   pl.BlockSpec((None, tq * dilation, LANES), out_map)),
        scratch_shapes=[pltpu.VMEM((GROUP_WIDTH // LANES, tq * dilation, LANES), F32)],
        compiler_params=_params(("parallel", "parallel", "arbitrary")),
        name=f"band_attn_d{dilation}",
    )(qv, kv, kv, kv, vv, vv, vv)
    return out.reshape(batch * seq_len, GROUP_WIDTH), lse.reshape(batch * seq_len, LANES)


def _ssd_kernel(xp_ref, bp_ref, cp_ref, dt_ref, cwx_ref, cwb_ref, cwc_ref, cbx_ref, cbb_ref, cbc_ref,
                alog_ref, dsk_ref, y_ref, xs_s, bm_s, cm_s, bt_s, hb_s, hf_s, hbc_s, acs_s, inj_s, dec_s, cols_s, *,
                seq_len):
    L = CHUNK
    nc = seq_len // L
    heads = HEADS_PER_SSM_GROUP
    pair_w = 2 * HEAD_DIM

    def conv_chunk(c, carry):
        r0 = pl.multiple_of(c * L, L)
        lo = pl.multiple_of(jnp.maximum(r0 - CONV_HALO, 0), CONV_HALO)
        hi = pl.multiple_of(jnp.minimum(r0 + L, seq_len - CONV_HALO), CONV_HALO)
        for src, w_ref, b_ref, dst in ((xp_ref, cwx_ref, cbx_ref, xs_s), (bp_ref, cwb_ref, cbb_ref, bm_s),
                                       (cp_ref, cwc_ref, cbc_ref, cm_s)):
            cur = src[pl.ds(r0, L), :].astype(F32)
            before = jnp.where(c > 0, src[pl.ds(lo, CONV_HALO), :].astype(F32), 0.0)
            after = jnp.where(c < nc - 1, src[pl.ds(hi, CONV_HALO), :].astype(F32), 0.0)
            ext = jnp.concatenate([before, cur, after], axis=0)
            w = w_ref[...]
            acc = b_ref[...] + ext[CONV_HALO - 2:CONV_HALO - 2 + L] * w[0:1]
            for tap in range(1, CONV_WIDTH):
                acc = acc + ext[CONV_HALO - 2 + tap:CONV_HALO - 2 + tap + L] * w[tap:tap + 1]
            act = acc * jax.nn.sigmoid(acc)
            dst[pl.ds(r0, L), :] = act.astype(BF16)
            if dst is bm_s:
                bt_s[c] = jnp.transpose(act).astype(BF16)
        return carry

    rows8 = 2 * heads
    dt_all = dt_ref[...]
    a_rows = -jnp.exp(alog_ref[...])
    dta = dt_all * jnp.concatenate([a_rows] * nc, axis=0)
    lane_all = lax.broadcasted_iota(jnp.int32, dta.shape, 1)
    fwd_rows = lax.broadcasted_iota(jnp.int32, dta.shape, 0) % rows8 < heads
    cs = dta
    shift = 1
    while shift < L:
        cs = cs + jnp.where(lane_all >= shift, pltpu.roll(cs, shift, axis=1), 0.0)
        shift *= 2
    total = cs[:, L - 1:L]
    acs_all = jnp.where(fwd_rows, cs, total - cs + dta)
    acs_s[...] = acs_all
    inj_s[...] = dt_all * jnp.exp(total - acs_all)
    dec_s[...] = jnp.broadcast_to(jnp.exp(total), dta.shape)
    pad_rows = jnp.zeros((L - rows8, L), F32)

    def column_chunk(c, carry):
        acs = acs_s[pl.ds(pl.multiple_of(c * rows8, rows8), rows8), :]
        cols_s[c] = jnp.transpose(jnp.concatenate([acs, pad_rows], axis=0))
        return carry

    def prepass(c, carry):
        return column_chunk(c, conv_chunk(c, carry))

    lax.fori_loop(0, nc, prepass, 0, unroll=2)

    hf_s[...] = jnp.zeros_like(hf_s)
    hbc_s[...] = jnp.zeros_like(hbc_s)
    dskip = dsk_ref[...]
    row = lax.broadcasted_iota(jnp.int32, (L, L), 0)
    col = lax.broadcasted_iota(jnp.int32, (L, L), 1)
    lower, upper = row >= col, row <= col
    low_lanes = lax.broadcasted_iota(jnp.int32, (L, pair_w), 1) < HEAD_DIM
    state_lane = lax.broadcasted_iota(jnp.int32, (1, pair_w), 1) < HEAD_DIM

    def decays(c):
        rows = pl.ds(pl.multiple_of(c * rows8, rows8), rows8)
        return dt_ref[rows, :], acs_s[rows, :], dec_s[rows, 0:1], inj_s[rows, :]

    def split_heads(xp):
        return jnp.concatenate([jnp.where(low_lanes, xp, 0), jnp.where(low_lanes, 0, xp)], axis=0)

    def state_update(h_s, c, r0, chunk_decay, inject, base):
        bt = bt_s[c].astype(F32)
        for pr in range(heads // 2):
            sl = slice(pr * pair_w, (pr + 1) * pair_w)
            hr = base + 2 * pr
            lhs = jnp.concatenate([(bt * inject[hr:hr + 1]).astype(BF16),
                                   (bt * inject[hr + 1:hr + 2]).astype(BF16)], axis=1)
            st = _dot(lhs, split_heads(xs_s[pl.ds(r0, L), sl]))
            dec = jnp.where(state_lane, chunk_decay[hr:hr + 1], chunk_decay[hr + 1:hr + 2])
            h_s[:, sl] = h_s[:, sl] * dec + st

    def sweep_a(i, carry):
        c = nc - 1 - i
        r0 = pl.multiple_of(c * L, L)
        _, _, chunk_decay, inject = decays(c)
        hb_s[c] = hbc_s[...].astype(BF16)
        state_update(hbc_s, c, r0, chunk_decay, inject, heads)
        return carry

    def sweep_b(c, carry):
        r0 = pl.multiple_of(c * L, L)
        dt8, acs, chunk_decay, inject = decays(c)
        cols = cols_s[c]
        cm = cm_s[pl.ds(r0, L), :]
        cm32 = cm.astype(F32)
        cb = _dot_nt(cm, bm_s[pl.ds(r0, L), :])
        hf = hf_s[...].astype(BF16)
        hb = hb_s[c]
        for pr in range(heads // 2):
            sl = slice(pr * pair_w, (pr + 1) * pair_w)
            w_sum, e_fwd, e_bwd = [], [], []
            for e in (2 * pr, 2 * pr + 1):
                col_f = jnp.broadcast_to(cols[:, e:e + 1], (L, L))
                col_b = jnp.broadcast_to(cols[:, heads + e:heads + e + 1], (L, L))
                lf = jnp.exp(jnp.where(lower, col_f - acs[e:e + 1], -jnp.inf)) * dt8[e:e + 1]
                lb = (jnp.exp(jnp.where(upper, col_b - acs[heads + e:heads + e + 1], -jnp.inf))
                      * dt8[heads + e:heads + e + 1])
                w_sum.append((cb * (lf + lb)).astype(BF16))
                e_fwd.append((cm32 * jnp.exp(col_f)).astype(BF16))
                e_bwd.append((cm32 * jnp.exp(col_b)).astype(BF16))
            xp = xs_s[pl.ds(r0, L), sl]
            y = _dot(jnp.concatenate(w_sum, axis=1), split_heads(xp))
            y = y + _dot(jnp.concatenate(e_fwd, axis=1), split_heads(hf[:, sl]))
            y = y + _dot(jnp.concatenate(e_bwd, axis=1), split_heads(hb[:, sl]))
            y = y + xp.astype(F32) * dskip[:, sl]
            y_ref[pl.ds(r0, L), sl] = y.astype(BF16)
        state_update(hf_s, c, r0, chunk_decay, inject, 0)
        return carry

    lax.fori_loop(0, nc, sweep_a, 0, unroll=4)
    lax.fori_loop(0, nc, sweep_b, 0, unroll=2)


def _ssd_call(xbc, dt, conv_w, conv_b, a_log, d_skip, batch, seq_len):
    b_off = D_INNER // D_STATE
    c_off = b_off + SSM_GROUPS
    nc = seq_len // CHUNK
    seq_block = lambda w, m: pl.BlockSpec((None, seq_len, w), m)
    rows = 2 * HEADS_PER_SSM_GROUP
    return pl.pallas_call(
        functools.partial(_ssd_kernel, seq_len=seq_len),
        out_shape=jax.ShapeDtypeStruct((batch, seq_len, D_INNER), BF16),
        grid=(batch, SSM_GROUPS),
        in_specs=[seq_block(SSM_GROUP_WIDTH, lambda b, g: (b, 0, g)),
                  seq_block(D_STATE, lambda b, g: (b, 0, b_off + g)),
                  seq_block(D_STATE, lambda b, g: (b, 0, c_off + g)),
                  pl.BlockSpec((None, None, nc * rows, CHUNK), lambda b, g: (g, b, 0, 0)),
                  pl.BlockSpec((CONV_WIDTH, SSM_GROUP_WIDTH), lambda b, g: (0, g)),
                  pl.BlockSpec((CONV_WIDTH, D_STATE), lambda b, g: (0, b_off + g)),
                  pl.BlockSpec((CONV_WIDTH, D_STATE), lambda b, g: (0, c_off + g)),
                  pl.BlockSpec((1, SSM_GROUP_WIDTH), lambda b, g: (0, g)),
                  pl.BlockSpec((1, D_STATE), lambda b, g: (0, b_off + g)),
                  pl.BlockSpec((1, D_STATE), lambda b, g: (0, c_off + g)),
                  pl.BlockSpec((None, rows, 1), lambda b, g: (g, 0, 0)),
                  pl.BlockSpec((None, 1, SSM_GROUP_WIDTH), lambda b, g: (g, 0, 0))],
        out_specs=seq_block(SSM_GROUP_WIDTH, lambda b, g: (b, 0, g)),
        scratch_shapes=[pltpu.VMEM((seq_len, SSM_GROUP_WIDTH), BF16), pltpu.VMEM((seq_len, D_STATE), BF16),
                        pltpu.VMEM((seq_len, D_STATE), BF16), pltpu.VMEM((nc, D_STATE, CHUNK), BF16),
                        pltpu.VMEM((nc, D_STATE, SSM_GROUP_WIDTH), BF16),
                        pltpu.VMEM((D_STATE, SSM_GROUP_WIDTH), F32), pltpu.VMEM((D_STATE, SSM_GROUP_WIDTH), F32),
                        pltpu.VMEM((nc * rows, CHUNK), F32), pltpu.VMEM((nc * rows, CHUNK), F32),
                        pltpu.VMEM((nc * rows, CHUNK), F32), pltpu.VMEM((nc, CHUNK, LANES), F32)],
        compiler_params=_params(("parallel", "parallel")),
        name="conv_ssd",
    )(xbc, xbc, xbc, dt, conv_w, conv_w, conv_w, conv_b, conv_b, conv_b, a_log, d_skip)


def _merge_kernel(x_ref, o0_ref, o1_ref, o2_ref, l0_ref, l1_ref, l2_ref, y_ref, z_ref, nw_ref, sn_ref,
                  wga_ref, wgm_ref, wab_ref, wsb_ref, wo_ref, ex_ref, h_ref):
    x = x_ref[...]
    u = _rms_scale(x, nw_ref[...]).astype(BF16)
    lses = (l0_ref[...], l1_ref[...], l2_ref[...])
    top = jnp.maximum(jnp.maximum(lses[0], lses[1]), lses[2])
    es = [jnp.exp(l - top) for l in lses]
    inv = 1.0 / (es[0] + es[1] + es[2])
    expand = ex_ref[...]
    mix = jnp.zeros((x.shape[0], GROUP_WIDTH), F32)
    for e, o_ref in zip(es, (o0_ref, o1_ref, o2_ref)):
        alpha = e * inv
        hi = alpha.astype(BF16)
        lo = (alpha - hi.astype(F32)).astype(BF16)
        mix = mix + (_dot(hi, expand) + _dot(lo, expand)) * o_ref[...].astype(F32)
    o_attn = _dot(mix.astype(BF16), wab_ref[...])
    z = z_ref[...].astype(F32)
    yz = y_ref[...].astype(F32) * (z * jax.nn.sigmoid(z))
    o_ssm = _dot(_rms_scale(yz, sn_ref[...]).astype(BF16), wsb_ref[...])
    merged = jax.nn.sigmoid(_dot(u, wga_ref[...])) * o_attn + jax.nn.sigmoid(_dot(u, wgm_ref[...])) * o_ssm
    h_ref[...] = x + _dot(merged.astype(BF16), wo_ref[...])


def _merge_call(x2, outs, lses, y, z, nw, ssm_norm, w_ga, w_gm, w_ab, w_sb, w_out, expand):
    tokens = x2.shape[0]
    tm = MERGE_TOKENS
    tok = lambda w: pl.BlockSpec((tm, w), lambda i: (i, 0))
    return pl.pallas_call(
        _merge_kernel,
        out_shape=jax.ShapeDtypeStruct((tokens, D_MODEL), F32),
        grid=(tokens // tm,),
        in_specs=[tok(D_MODEL), tok(GROUP_WIDTH), tok(GROUP_WIDTH), tok(GROUP_WIDTH),
                  tok(LANES), tok(LANES), tok(LANES), tok(D_INNER), tok(D_INNER),
                  _resident((1, D_MODEL)), _resident((1, D_INNER)),
                  _resident((D_MODEL, D_MODEL)), _resident((D_MODEL, D_MODEL)),
                  _resident((GROUP_WIDTH, D_MODEL)), _resident((D_INNER, D_MODEL)),
                  _resident((D_MODEL, D_MODEL)), _resident((LANES, GROUP_WIDTH))],
        out_specs=tok(D_MODEL),
        compiler_params=_params(("parallel",)),
        name="merge_out",
    )(x2, *outs, *lses, y, z, nw, ssm_norm, w_ga, w_gm, w_ab, w_sb, w_out, expand)


def _mlp_kernel(h_ref, nw_ref, w1_ref, w2_ref, nf_ref, o_ref):
    h = h_ref[...]
    u = _rms_scale(h, nw_ref[...]).astype(BF16)
    acc = h
    for c in range(D_FF // MATMUL_COLS):
        sl = slice(c * MATMUL_COLS, (c + 1) * MATMUL_COLS)
        a = jnp.maximum(_dot(u, w1_ref[:, sl]), 0.0)
        acc = acc + _dot((a * a).astype(BF16), w2_ref[sl, :])
    o_ref[...] = _rms_scale(acc, nf_ref[...])


def _mlp_call(h, nw, w1, w2, nf):
    tokens = h.shape[0]
    tm = MLP_TOKENS
    tok = pl.BlockSpec((tm, D_MODEL), lambda i: (i, 0))
    return pl.pallas_call(
        _mlp_kernel,
        out_shape=jax.ShapeDtypeStruct((tokens, D_MODEL), F32),
        grid=(tokens // tm,),
        in_specs=[tok, _resident((1, D_MODEL)), _resident((D_MODEL, D_FF)), _resident((D_FF, D_MODEL)),
                  _resident((1, D_MODEL))],
        out_specs=tok,
        compiler_params=_params(("parallel",)),
        name="mlp_norm",
    )(h, nw, w1, w2, nf)


def _rope_tables(seq_len):
    half = HEAD_DIM // 2
    inv = ROPE_THETA ** (-(jnp.arange(half, dtype=F32) * 2.0 / HEAD_DIM))
    ang = jnp.arange(seq_len, dtype=F32)[:, None] * inv[None, :]
    cos, sin = jnp.cos(ang), jnp.sin(ang)
    reps = LANES // HEAD_DIM
    return (jnp.tile(jnp.concatenate([cos, cos], axis=-1), (1, reps)),
            jnp.tile(jnp.concatenate([-sin, sin], axis=-1), (1, reps)))


def _trunk(x, p):
    batch, seq_len, _ = x.shape
    x2 = x.reshape(batch * seq_len, D_MODEL)
    cos, sin = _rope_tables(seq_len)
    qkv = _qkv_call(x2, p["norm_mix"], p["w_qkv"], cos, sin, seq_len)
    z, xbc, dt = _ssmproj_call(x2, p["norm_mix"], p["w_z"], p["w_xbc"], p["w_dt_t"], p["dt_bias"], batch, seq_len)
    outs, lses = [], []
    groups = len(DILATIONS)
    for group, dilation in enumerate(DILATIONS):
        o, lse = _attn_call(qkv[group], qkv[groups + group], qkv[2 * groups + group], dilation, batch, seq_len)
        outs.append(o)
        lses.append(lse)
    y = _ssd_call(xbc.reshape(batch, seq_len, CONV_DIM), dt, p["conv_w"], p["conv_b"], p["a_log"], p["d_skip"],
                  batch, seq_len)
    h = _merge_call(x2, outs, lses, y.reshape(batch * seq_len, D_INNER), z, p["norm_mix"], p["ssm_norm"],
                    p["w_ga"], p["w_gm"], p["w_ab"], p["w_sb"], p["w_out"], p["expand"])
    out = _mlp_call(h, p["norm_mlp"], p["w_mlp_in"], p["w_mlp_out"], p["norm_final"])
    return out.reshape(batch, seq_len, D_MODEL)


def _prepare(norm_mix, w_in, conv_w, conv_b, dt_bias, a_log, d_skip, ssm_norm, w_attn_branch, w_ssm_branch,
             w_out, norm_mlp, w_mlp_in, w_mlp_out, norm_final):
    w = w_in[0]
    bounds = {}
    start = 0
    for name, width in (("qkv", 3 * QKV_WIDTH), ("z", D_INNER), ("xbc", CONV_DIM), ("dt", 2 * SSM_HEADS),
                        ("ga", D_MODEL), ("gm", D_MODEL)):
        bounds[name] = (start, start + width)
        start += width
    cut = lambda name: w[:, bounds[name][0]:bounds[name][1]]
    perm = jnp.arange(2 * SSM_HEADS).reshape(2, SSM_GROUPS, HEADS_PER_SSM_GROUP).transpose(1, 0, 2).reshape(-1)
    heads_lanes = jnp.arange(LANES)[:, None] == (jnp.arange(GROUP_WIDTH)[None, :] // HEAD_DIM)
    return {
        "norm_mix": norm_mix[0][None, :],
        "w_qkv": cut("qkv").astype(BF16),
        "w_z": cut("z").astype(BF16),
        "w_xbc": cut("xbc").astype(BF16),
        "w_dt_t": cut("dt")[:, perm].T,
        "dt_bias": dt_bias[0].reshape(-1)[perm][:, None],
        "a_log": a_log[0].reshape(-1)[perm].reshape(SSM_GROUPS, 2 * HEADS_PER_SSM_GROUP, 1),
        "d_skip": jnp.repeat(d_skip[0].reshape(SSM_GROUPS, HEADS_PER_SSM_GROUP), HEAD_DIM, axis=1)[:, None, :],
        "conv_w": conv_w[0],
        "conv_b": conv_b[0][None, :],
        "ssm_norm": ssm_norm[0][None, :],
        "w_ga": cut("ga").astype(BF16),
        "w_gm": cut("gm").astype(BF16),
        "w_ab": w_attn_branch[0].astype(BF16),
        "w_sb": w_ssm_branch[0].astype(BF16),
        "w_out": w_out[0].astype(BF16),
        "expand": heads_lanes.astype(BF16),
        "norm_mlp": norm_mlp[0][None, :],
        "w_mlp_in": w_mlp_in[0].astype(BF16),
        "w_mlp_out": w_mlp_out[0].astype(BF16),
        "norm_final": norm_final[None, :],
    }


def kernel(x_prompt, x_sample, norm_mix, w_in, conv_w, conv_b, dt_bias, a_log, d_skip, ssm_norm, w_attn_branch,
           w_ssm_branch, w_out, norm_mlp, w_mlp_in, w_mlp_out, norm_final):
    p = _prepare(norm_mix, w_in, conv_w, conv_b, dt_bias, a_log, d_skip, ssm_norm, w_attn_branch, w_ssm_branch,
                 w_out, norm_mlp, w_mlp_in, w_mlp_out, norm_final)
    return _trunk(x_prompt, p), _trunk(x_sample, p)
```

```python
import functools
import math

import jax
import jax.numpy as jnp
from jax import lax
from jax.experimental import pallas as pl
from jax.experimental.pallas import tpu as pltpu

F32 = jnp.float32
BF16 = jnp.bfloat16

D_MODEL = 1024
HEAD_DIM = 64
DILATIONS = (1, 4, 16)
RADIUS = 64
HEADS_PER_GROUP = 8
GROUP_WIDTH = HEADS_PER_GROUP * HEAD_DIM
QKV_WIDTH = len(DILATIONS) * GROUP_WIDTH
ROPE_THETA = 10000.0
D_INNER = 2048
SSM_GROUPS = 8
HEADS_PER_SSM_GROUP = 4
SSM_HEADS = SSM_GROUPS * HEADS_PER_SSM_GROUP
SSM_GROUP_WIDTH = HEADS_PER_SSM_GROUP * HEAD_DIM
D_STATE = 128
CONV_WIDTH = 5
CONV_DIM = D_INNER + 2 * SSM_GROUPS * D_STATE
CHUNK = 128
D_FF = 4096
RMS_EPS = 1e-6
NEG_INF = -1e30

LANES = 128
CONV_HALO = 64
CONV_ROWS = 32
LOG2_E = math.log2(math.e)
VMEM_LIMIT_BYTES = 56 * 1024 * 1024

QKV_TOKENS = 512
SSMPROJ_TOKENS = 512
MERGE_TOKENS = 512
MLP_TOKENS = 512
ATTN_QUERY_TILE = {1: 512, 4: 256, 16: 128}
ATTN_QUERY_BLOCK = 128
MATMUL_COLS = 512


def _params(semantics):
    return pltpu.CompilerParams(dimension_semantics=semantics, vmem_limit_bytes=VMEM_LIMIT_BYTES)


def _resident(shape):
    zeros = (0,) * len(shape)
    return pl.BlockSpec(shape, lambda *_: zeros, pipeline_mode=pl.Buffered(1))


def _rms_scale(x, w):
    return x * lax.rsqrt(jnp.mean(x * x, axis=-1, keepdims=True) + RMS_EPS) * w


def _dot(a, b):
    return jnp.dot(a, b, preferred_element_type=F32)


def _dot_nt(a, b):
    return lax.dot_general(a, b, (((1,), (1,)), ((), ())), preferred_element_type=F32)


def _qkv_kernel(x_ref, nw_ref, w_ref, cos_ref, sin_ref, *refs):
    out_refs, stage = refs[:-1], refs[-1]
    tm = x_ref.shape[0]
    u = _rms_scale(x_ref[...], nw_ref[...]).astype(BF16)
    cos = cos_ref[...]
    sin = sin_ref[...]
    lane = lax.broadcasted_iota(jnp.int32, cos.shape, 1)
    first_half = (lane % HEAD_DIM) < (HEAD_DIM // 2)

    def rope(p):
        partner = jnp.where(first_half, pltpu.roll(p, LANES - HEAD_DIM // 2, axis=1),
                            pltpu.roll(p, HEAD_DIM // 2, axis=1))
        return p * cos + partner * sin

    for seg, (rotary, scale) in enumerate(((True, LOG2_E / math.sqrt(HEAD_DIM)), (True, 1.0), (False, 1.0))):
        for group, dilation in enumerate(DILATIONS):
            o_ref = out_refs[seg * len(DILATIONS) + group]
            col = seg * QKV_WIDTH + group * GROUP_WIDTH
            acc = _dot(u, w_ref[:, col:col + GROUP_WIDTH])
            for t in range(GROUP_WIDTH // LANES):
                piece = acc[:, t * LANES:(t + 1) * LANES]
                if rotary:
                    piece = rope(piece) * scale
                if dilation == 1:
                    o_ref[:, t * LANES:(t + 1) * LANES] = piece.astype(BF16)
                else:
                    stage[t] = piece
                    for r in range(dilation):
                        o_ref[:, r * GROUP_WIDTH + t * LANES:r * GROUP_WIDTH + (t + 1) * LANES] = (
                            stage[t, pl.ds(r, tm // dilation, stride=dilation), :].astype(BF16))


def _qkv_call(x2, nw, w_qkv, cos, sin, seq_len):
    tokens = x2.shape[0]
    tm = QKV_TOKENS
    pos_blocks = seq_len // tm
    tok = lambda i: (i, 0)
    pos = lambda i: (i % pos_blocks, 0)
    shapes, specs = [], []
    for _ in range(3):
        for d in DILATIONS:
            shapes.append(jax.ShapeDtypeStruct((tokens // d, d * GROUP_WIDTH), BF16))
            specs.append(pl.BlockSpec((tm // d, d * GROUP_WIDTH), tok))
    return pl.pallas_call(
        _qkv_kernel,
        out_shape=tuple(shapes),
        grid=(tokens // tm,),
        in_specs=[pl.BlockSpec((tm, D_MODEL), tok), _resident((1, D_MODEL)),
                  _resident((D_MODEL, 3 * QKV_WIDTH)),
                  pl.BlockSpec((tm, LANES), pos), pl.BlockSpec((tm, LANES), pos)],
        out_specs=tuple(specs),
        scratch_shapes=[pltpu.VMEM((GROUP_WIDTH // LANES, tm, LANES), F32)],
        compiler_params=_params(("parallel",)),
        name="qkv_proj",
    )(x2, nw, w_qkv, cos, sin)


def _softplus(x):
    return jnp.maximum(x, 0.0) + jnp.log1p(jnp.exp(-jnp.abs(x)))


def _ssmproj_kernel(x_ref, nw_ref, wz_ref, wx_ref, wdt_ref, dtb_ref, z_ref, xbc_ref, dt_ref):
    u32 = _rms_scale(x_ref[...], nw_ref[...])
    u = u32.astype(BF16)
    for c in range(D_INNER // MATMUL_COLS):
        sl = slice(c * MATMUL_COLS, (c + 1) * MATMUL_COLS)
        z_ref[:, sl] = _dot(u, wz_ref[:, sl]).astype(BF16)
    for c in range(CONV_DIM // MATMUL_COLS):
        sl = slice(c * MATMUL_COLS, (c + 1) * MATMUL_COLS)
        xbc_ref[:, sl] = _dot(u, wx_ref[:, sl]).astype(BF16)
    dt_t = lax.dot_general(wdt_ref[...], u32, (((1,), (1,)), ((), ())),
                           precision=lax.Precision.HIGHEST, preferred_element_type=F32)
    dt_t = _softplus(dt_t + dtb_ref[...])
    rows = 2 * HEADS_PER_SSM_GROUP
    for g in range(SSM_GROUPS):
        for c in range(dt_ref.shape[1] // rows):
            dt_ref[g, c * rows:(c + 1) * rows, :] = dt_t[g * rows:(g + 1) * rows, c * CHUNK:(c + 1) * CHUNK]


def _ssmproj_call(x2, nw, w_z, w_xbc, w_dt_t, dt_bias, batch, seq_len):
    tokens = x2.shape[0]
    tm = SSMPROJ_TOKENS
    per_seq = seq_len // tm
    rows = 2 * HEADS_PER_SSM_GROUP
    tok = lambda i: (i, 0)
    return pl.pallas_call(
        _ssmproj_kernel,
        out_shape=(jax.ShapeDtypeStruct((tokens, D_INNER), BF16),
                   jax.ShapeDtypeStruct((tokens, CONV_DIM), BF16),
                   jax.ShapeDtypeStruct((SSM_GROUPS, batch, seq_len // CHUNK * rows, CHUNK), F32)),
        grid=(tokens // tm,),
        in_specs=[pl.BlockSpec((tm, D_MODEL), tok), _resident((1, D_MODEL)),
                  _resident((D_MODEL, D_INNER)), _resident((D_MODEL, CONV_DIM)),
                  _resident((2 * SSM_HEADS, D_MODEL)), _resident((2 * SSM_HEADS, 1))],
        out_specs=(pl.BlockSpec((tm, D_INNER), tok), pl.BlockSpec((tm, CONV_DIM), tok),
                   pl.BlockSpec((SSM_GROUPS, None, tm // CHUNK * rows, CHUNK),
                                lambda i: (0, i // per_seq, i % per_seq, 0))),
        compiler_params=_params(("parallel",)),
        name="ssm_proj",
    )(x2, nw, w_z, w_xbc, w_dt_t, dt_bias)


def _attn_kernel(q_ref, kp_ref, km_ref, kn_ref, vp_ref, vm_ref, vn_ref, o_ref, lse_ref, o_s, k_s, v_s, *, tq,
                 sub_len, dilation):
    j = pl.program_id(1)
    r = pl.program_id(2)
    for dst, parts in ((k_s, (kp_ref, km_ref, kn_ref)), (v_s, (vp_ref, vm_ref, vn_ref))):
        dst[0:RADIUS] = parts[0][...]
        dst[RADIUS:RADIUS + tq] = parts[1][...]
        dst[RADIUS + tq:] = parts[2][...]
    qb = min(ATTN_QUERY_BLOCK, tq)
    kb = qb + 2 * RADIUS
    row = lax.broadcasted_iota(jnp.int32, (qb, kb), 0)
    col = lax.broadcasted_iota(jnp.int32, (qb, kb), 1)
    band = (col >= row) & (col <= row + 2 * RADIUS)
    head_lane = lax.broadcasted_iota(jnp.int32, (qb, LANES), 1)
    low_q = lax.broadcasted_iota(jnp.int32, (qb, LANES), 1) < HEAD_DIM

    def query_block(sb, carry):
        base = pl.multiple_of(sb * qb, qb)
        key_pos = j * tq + base - RADIUS + col
        valid = band & (key_pos >= 0) & (key_pos < sub_len)
        rows = pl.ds(r + base * dilation, qb, stride=dilation) if dilation > 1 else pl.ds(base, qb)
        lse_tile = jnp.zeros((qb, LANES), F32)
        for pair in range(HEADS_PER_GROUP // 2):
            sl = slice(pair * LANES, (pair + 1) * LANES)
            qp = q_ref[pl.ds(base, qb), sl]
            kp = k_s[pl.ds(base, kb), sl]
            vp = v_s[pl.ds(base, kb), sl]
            halves = []
            for half in range(2):
                keep_q = low_q if half == 0 else ~low_q
                s = jnp.where(valid, _dot_nt(jnp.where(keep_q, qp, 0), kp), NEG_INF)
                m = jnp.max(s, axis=-1, keepdims=True)
                p = jnp.exp2(s - m)
                l = jnp.sum(p, axis=-1, keepdims=True)
                halves.append(_dot(p.astype(BF16), vp) / l)
                lse_tile = jnp.where(head_lane == 2 * pair + half, (m + jnp.log2(l)) * (1.0 / LOG2_E), lse_tile)
            o_s[pair, rows, :] = jnp.where(low_q, halves[0], halves[1])
        lse_ref[rows, :] = lse_tile
        return carry

    lax.fori_loop(0, tq // qb, query_block, 0, unroll=True)

    @pl.when(r == dilation - 1)
    def _():
        for pair in range(HEADS_PER_GROUP // 2):
            o_ref[:, pair * LANES:(pair + 1) * LANES] = o_s[pair].astype(BF16)


def _attn_call(q, k, v, dilation, batch, seq_len):
    sub_len = seq_len // dilation
    tq = min(ATTN_QUERY_TILE[dilation], sub_len)
    halo_per_tile = tq // RADIUS
    last_halo = sub_len // RADIUS - 1
    view = lambda a: a.reshape(batch, sub_len, dilation * GROUP_WIDTH)
    main = lambda b, j, r: (b, j, r)
    prev = lambda b, j, r: (b, jnp.maximum(j * halo_per_tile - 1, 0), r)
    nxt = lambda b, j, r: (b, jnp.minimum((j + 1) * halo_per_tile, last_halo), r)
    out_map = lambda b, j, r: (b, j, 0)
    tile = lambda m: pl.BlockSpec((None, tq, GROUP_WIDTH), m)
    halo = lambda m: pl.BlockSpec((None, RADIUS, GROUP_WIDTH), m)
    qv, kv, vv = view(q), view(k), view(v)
    out, lse = pl.pallas_call(
        functools.partial(_attn_kernel, tq=tq, sub_len=sub_len, dilation=dilation),
        out_shape=(jax.ShapeDtypeStruct((batch, seq_len, GROUP_WIDTH), BF16),
                   jax.ShapeDtypeStruct((batch, seq_len, LANES), F32)),
        grid=(batch, sub_len // tq, dilation),
        in_specs=[tile(main), halo(prev), tile(main), halo(nxt), halo(prev), tile(main), halo(nxt)],
        out_specs=(pl.BlockSpec((None, tq * dilation, GROUP_WIDTH), out_map),
                   pl.BlockSpec((None, tq * dilation, LANES), out_map)),
        scratch_shapes=[pltpu.VMEM((GROUP_WIDTH // LANES, tq * dilation, LANES), F32),
                        pltpu.VMEM((tq + 2 * RADIUS, GROUP_WIDTH), BF16),
                        pltpu.VMEM((tq + 2 * RADIUS, GROUP_WIDTH), BF16)],
        compiler_params=_params(("parallel", "parallel", "arbitrary")),
        name=f"band_attn_d{dilation}",
    )(qv, kv, kv, kv, vv, vv, vv)
    return out.reshape(batch * seq_len, GROUP_WIDTH), lse.reshape(batch * seq_len, LANES)


def _ssd_kernel(xp_ref, bp_ref, cp_ref, dt_ref, cwx_ref, cwb_ref, cwc_ref, cbx_ref, cbb_ref, cbc_ref,
                alog_ref, dsk_ref, y_ref, xs_s, bm_s, cm_s, bt_s, hb_s, hf_s, hbc_s, acs_s, inj_s, dec_s, cols_s, shift_s, *,
                seq_len):
    L = CHUNK
    nc = seq_len // L
    heads = HEADS_PER_SSM_GROUP
    pair_w = 2 * HEAD_DIM

    side_taps = [t for t in range(CONV_WIDTH) if t != CONV_WIDTH // 2]
    sel_row = lax.broadcasted_iota(jnp.int32, (len(side_taps) * L, 2 * L), 0)
    sel_col = lax.broadcasted_iota(jnp.int32, (len(side_taps) * L, 2 * L), 1)
    slab = len(side_taps) * CONV_ROWS
    tap_slot = sel_row % slab // CONV_ROWS
    tap_offset = jnp.where(tap_slot < CONV_WIDTH // 2, tap_slot, tap_slot + 1) - CONV_WIDTH // 2
    out_row = sel_row // slab * CONV_ROWS + sel_row % CONV_ROWS
    shift_s[...] = (sel_col == CONV_HALO + out_row + tap_offset).astype(BF16)

    def conv_chunk(c, carry):
        r0 = pl.multiple_of(c * L, L)
        lo = pl.multiple_of(jnp.maximum(r0 - CONV_HALO, 0), CONV_HALO)
        hi = pl.multiple_of(jnp.minimum(r0 + L, seq_len - CONV_HALO), CONV_HALO)
        shifts = shift_s[...]

        def window(rows, size, keep):
            parts = [src[pl.ds(rows, size), :] for src in (xp_ref, bp_ref, cp_ref)]
            parts = [jnp.where(keep, part, jnp.zeros_like(part)) for part in parts]
            return parts[0], jnp.concatenate(parts[1:], axis=1)

        windows = (window(lo, CONV_HALO, c > 0), window(r0, L, True), window(hi, CONV_HALO, c < nc - 1))
        weights = (cwx_ref[...], jnp.concatenate([cwb_ref[...], cwc_ref[...]], axis=1))
        biases = (cbx_ref[...], jnp.concatenate([cbb_ref[...], cbc_ref[...]], axis=1))
        acts = []
        for half in range(2):
            before, cur, after = (windows[part][half] for part in range(3))
            shifted = _dot(shifts, jnp.concatenate([before, cur, after], axis=0))
            w = weights[half]
            blocks = []
            for blk in range(L // CONV_ROWS):
                rows = slice(blk * CONV_ROWS, (blk + 1) * CONV_ROWS)
                acc = biases[half] + cur[rows].astype(F32) * w[CONV_WIDTH // 2:CONV_WIDTH // 2 + 1]
                for slot, tap in enumerate(side_taps):
                    start = (blk * len(side_taps) + slot) * CONV_ROWS
                    acc = acc + shifted[start:start + CONV_ROWS] * w[tap:tap + 1]
                blocks.append(acc * jax.nn.sigmoid(acc))
            acts.append(jnp.concatenate(blocks, axis=0))
        xs_s[pl.ds(r0, L), :] = acts[0].astype(BF16)
        bm_s[pl.ds(r0, L), :] = acts[1][:, :D_STATE].astype(BF16)
        cm_s[pl.ds(r0, L), :] = acts[1][:, D_STATE:].astype(BF16)
        bt_s[c] = jnp.transpose(acts[1][:, :D_STATE]).astype(BF16)
        return carry

    rows8 = 2 * heads
    dt_all = dt_ref[...]
    a_rows = -jnp.exp(alog_ref[...]) * LOG2_E
    dta = dt_all * jnp.concatenate([a_rows] * nc, axis=0)
    fwd_rows = lax.broadcasted_iota(jnp.int32, dta.shape, 0) % rows8 < heads
    ones_upper = (lax.broadcasted_iota(jnp.int32, (L, L), 0) <= lax.broadcasted_iota(jnp.int32, (L, L), 1)).astype(BF16)
    cs = jnp.zeros(dta.shape, F32)
    rest = dta
    for _ in range(3):
        piece = rest.astype(BF16)
        cs = cs + _dot(piece, ones_upper)
        rest = rest - piece.astype(F32)
    total = cs[:, L - 1:L]
    acs_all = jnp.where(fwd_rows, cs, total - cs + dta)
    acs_s[...] = acs_all
    inj_s[...] = dt_all * jnp.exp2(total - acs_all)
    dec_s[...] = jnp.broadcast_to(jnp.exp2(total), dta.shape)
    pad_rows = jnp.zeros((L - rows8, L), F32)

    def column_chunk(c, carry):
        acs = acs_s[pl.ds(pl.multiple_of(c * rows8, rows8), rows8), :]
        cols_s[c] = jnp.transpose(jnp.concatenate([acs, pad_rows], axis=0))
        return carry

    def prepass(c, carry):
        return column_chunk(c, conv_chunk(c, carry))

    lax.fori_loop(0, nc, prepass, 0, unroll=8)

    hf_s[...] = jnp.zeros_like(hf_s)
    hbc_s[...] = jnp.zeros_like(hbc_s)
    dskip = dsk_ref[...]
    row = lax.broadcasted_iota(jnp.int32, (L, L), 0)
    col = lax.broadcasted_iota(jnp.int32, (L, L), 1)
    lower, strict_lower, strict_upper = row >= col, row > col, row < col
    low_lanes = lax.broadcasted_iota(jnp.int32, (L, pair_w), 1) < HEAD_DIM
    state_lane = lax.broadcasted_iota(jnp.int32, (1, pair_w), 1) < HEAD_DIM

    def decays(c):
        rows = pl.ds(pl.multiple_of(c * rows8, rows8), rows8)
        return dt_ref[rows, :], acs_s[rows, :], dec_s[rows, 0:1], inj_s[rows, :]

    def split_heads(xp):
        return jnp.concatenate([jnp.where(low_lanes, xp, 0), jnp.where(low_lanes, 0, xp)], axis=0)

    def state_update(h_s, c, r0, chunk_decay, inject, base):
        bt = bt_s[c].astype(F32)
        for pr in range(heads // 2):
            sl = slice(pr * pair_w, (pr + 1) * pair_w)
            hr = base + 2 * pr
            lhs = jnp.concatenate([(bt * inject[hr:hr + 1]).astype(BF16),
                                   (bt * inject[hr + 1:hr + 2]).astype(BF16)], axis=1)
            st = _dot(lhs, split_heads(xs_s[pl.ds(r0, L), sl]))
            dec = jnp.where(state_lane, chunk_decay[hr:hr + 1], chunk_decay[hr + 1:hr + 2])
            h_s[:, sl] = h_s[:, sl] * dec + st

    def sweep_a(i, carry):
        c = nc - 1 - i
        r0 = pl.multiple_of(c * L, L)
        _, _, chunk_decay, inject = decays(c)
        hb_s[c] = hbc_s[...].astype(BF16)
        state_update(hbc_s, c, r0, chunk_decay, inject, heads)
        return carry

    def sweep_b(c, carry):
        r0 = pl.multiple_of(c * L, L)
        dt8, acs, chunk_decay, inject = decays(c)
        cols = cols_s[c]
        cm = cm_s[pl.ds(r0, L), :]
        cb = _dot_nt(cm, bm_s[pl.ds(r0, L), :])
        from_fwd = _dot(cm, hf_s[...].astype(BF16))
        from_bwd = _dot(cm, hb_s[c])
        for pr in range(heads // 2):
            sl = slice(pr * pair_w, (pr + 1) * pair_w)
            w_sum, col_f, col_b = [], [], []
            for e in (2 * pr, 2 * pr + 1):
                fwd, bwd = slice(e, e + 1), slice(heads + e, heads + e + 1)
                col_f.append(jnp.broadcast_to(cols[:, fwd], (L, L)))
                col_b.append(jnp.broadcast_to(cols[:, bwd], (L, L)))
                decay = jnp.exp2(jnp.where(lower, col_f[-1] - acs[fwd], col_b[-1] - acs[bwd]))
                dt_sel = jnp.where(strict_lower, dt8[fwd], jnp.where(strict_upper, dt8[bwd], dt8[fwd] + dt8[bwd]))
                w_sum.append((cb * (decay * dt_sel)).astype(BF16))
            xp = xs_s[pl.ds(r0, L), sl]
            y = _dot(jnp.concatenate(w_sum, axis=1), split_heads(xp))
            y = y + from_fwd[:, sl] * jnp.exp2(jnp.where(low_lanes, col_f[0], col_f[1]))
            y = y + from_bwd[:, sl] * jnp.exp2(jnp.where(low_lanes, col_b[0], col_b[1]))
            y = y + xp.astype(F32) * dskip[:, sl]
            y_ref[pl.ds(r0, L), sl] = y.astype(BF16)
        state_update(hf_s, c, r0, chunk_decay, inject, 0)
        return carry

    lax.fori_loop(0, nc, sweep_a, 0, unroll=4)
    lax.fori_loop(0, nc, sweep_b, 0, unroll=8)


def _ssd_call(xbc, dt, conv_w, conv_b, a_log, d_skip, batch, seq_len):
    b_off = D_INNER // D_STATE
    c_off = b_off + SSM_GROUPS
    nc = seq_len // CHUNK
    seq_block = lambda w, m: pl.BlockSpec((None, seq_len, w), m)
    rows = 2 * HEADS_PER_SSM_GROUP
    return pl.pallas_call(
        functools.partial(_ssd_kernel, seq_len=seq_len),
        out_shape=jax.ShapeDtypeStruct((batch, seq_len, D_INNER), BF16),
        grid=(batch, SSM_GROUPS),
        in_specs=[seq_block(SSM_GROUP_WIDTH, lambda b, g: (b, 0, g)),
                  seq_block(D_STATE, lambda b, g: (b, 0, b_off + g)),
                  seq_block(D_STATE, lambda b, g: (b, 0, c_off + g)),
                  pl.BlockSpec((None, None, nc * rows, CHUNK), lambda b, g: (g, b, 0, 0)),
                  pl.BlockSpec((CONV_WIDTH, SSM_GROUP_WIDTH), lambda b, g: (0, g)),
                  pl.BlockSpec((CONV_WIDTH, D_STATE), lambda b, g: (0, b_off + g)),
                  pl.BlockSpec((CONV_WIDTH, D_STATE), lambda b, g: (0, c_off + g)),
                  pl.BlockSpec((1, SSM_GROUP_WIDTH), lambda b, g: (0, g)),
                  pl.BlockSpec((1, D_STATE), lambda b, g: (0, b_off + g)),
                  pl.BlockSpec((1, D_STATE), lambda b, g: (0, c_off + g)),
                  pl.BlockSpec((None, rows, 1), lambda b, g: (g, 0, 0)),
                  pl.BlockSpec((None, 1, SSM_GROUP_WIDTH), lambda b, g: (g, 0, 0))],
        out_specs=seq_block(SSM_GROUP_WIDTH, lambda b, g: (b, 0, g)),
        scratch_shapes=[pltpu.VMEM((seq_len, SSM_GROUP_WIDTH), BF16), pltpu.VMEM((seq_len, D_STATE), BF16),
                        pltpu.VMEM((seq_len, D_STATE), BF16), pltpu.VMEM((nc, D_STATE, CHUNK), BF16),
                        pltpu.VMEM((nc, D_STATE, SSM_GROUP_WIDTH), BF16),
                        pltpu.VMEM((D_STATE, SSM_GROUP_WIDTH), F32), pltpu.VMEM((D_STATE, SSM_GROUP_WIDTH), F32),
                        pltpu.VMEM((nc * rows, CHUNK), F32), pltpu.VMEM((nc * rows, CHUNK), F32),
                        pltpu.VMEM((nc * rows, CHUNK), F32), pltpu.VMEM((nc, CHUNK, LANES), F32),
                        pltpu.VMEM(((CONV_WIDTH - 1) * CHUNK, 2 * CHUNK), BF16)],
        compiler_params=_params(("parallel", "parallel")),
        name="conv_ssd",
    )(xbc, xbc, xbc, dt, conv_w, conv_w, conv_w, conv_b, conv_b, conv_b, a_log, d_skip)


def _merge_kernel(x_ref, o0_ref, o1_ref, o2_ref, l0_ref, l1_ref, l2_ref, y_ref, z_ref, nw_ref, sn_ref,
                  wga_ref, wgm_ref, wab_ref, wsb_ref, wo_ref, ex_ref, h_ref):
    x = x_ref[...]
    u = _rms_scale(x, nw_ref[...]).astype(BF16)
    lses = (l0_ref[...], l1_ref[...], l2_ref[...])
    top = jnp.maximum(jnp.maximum(lses[0], lses[1]), lses[2])
    es = [jnp.exp(l - top) for l in lses]
    inv = 1.0 / (es[0] + es[1] + es[2])
    expand = ex_ref[...]
    mix = jnp.zeros((x.shape[0], GROUP_WIDTH), F32)
    for e, o_ref in zip(es, (o0_ref, o1_ref, o2_ref)):
        alpha = e * inv
        hi = alpha.astype(BF16)
        lo = (alpha - hi.astype(F32)).astype(BF16)
        mix = mix + (_dot(hi, expand) + _dot(lo, expand)) * o_ref[...].astype(F32)
    o_attn = _dot(mix.astype(BF16), wab_ref[...])
    z = z_ref[...].astype(F32)
    yz = y_ref[...].astype(F32) * (z * jax.nn.sigmoid(z))
    o_ssm = _dot(_rms_scale(yz, sn_ref[...]).astype(BF16), wsb_ref[...])
    merged = jax.nn.sigmoid(_dot(u, wga_ref[...])) * o_attn + jax.nn.sigmoid(_dot(u, wgm_ref[...])) * o_ssm
    h_ref[...] = x + _dot(merged.astype(BF16), wo_ref[...])


def _merge_call(x2, outs, lses, y, z, nw, ssm_norm, w_ga, w_gm, w_ab, w_sb, w_out, expand):
    tokens = x2.shape[0]
    tm = MERGE_TOKENS
    tok = lambda w: pl.BlockSpec((tm, w), lambda i: (i, 0))
    return pl.pallas_call(
        _merge_kernel,
        out_shape=jax.ShapeDtypeStruct((tokens, D_MODEL), F32),
        grid=(tokens // tm,),
        in_specs=[tok(D_MODEL), tok(GROUP_WIDTH), tok(GROUP_WIDTH), tok(GROUP_WIDTH),
                  tok(LANES), tok(LANES), tok(LANES), tok(D_INNER), tok(D_INNER),
                  _resident((1, D_MODEL)), _resident((1, D_INNER)),
                  _resident((D_MODEL, D_MODEL)), _resident((D_MODEL, D_MODEL)),
                  _resident((GROUP_WIDTH, D_MODEL)), _resident((D_INNER, D_MODEL)),
                  _resident((D_MODEL, D_MODEL)), _resident((LANES, GROUP_WIDTH))],
        out_specs=tok(D_MODEL),
        compiler_params=_params(("parallel",)),
        name="merge_out",
    )(x2, *outs, *lses, y, z, nw, ssm_norm, w_ga, w_gm, w_ab, w_sb, w_out, expand)


def _mlp_kernel(h_ref, nw_ref, w1_ref, w2_ref, nf_ref, o_ref):
    h = h_ref[...]
    u = _rms_scale(h, nw_ref[...]).astype(BF16)
    acc = h
    for c in range(D_FF // MATMUL_COLS):
        sl = slice(c * MATMUL_COLS, (c + 1) * MATMUL_COLS)
        a = jnp.maximum(_dot(u, w1_ref[:, sl]), 0.0)
        acc = acc + _dot((a * a).astype(BF16), w2_ref[sl, :])
    o_ref[...] = _rms_scale(acc, nf_ref[...])


def _mlp_call(h, nw, w1, w2, nf):
    tokens = h.shape[0]
    tm = MLP_TOKENS
    tok = pl.BlockSpec((tm, D_MODEL), lambda i: (i, 0))
    return pl.pallas_call(
        _mlp_kernel,
        out_shape=jax.ShapeDtypeStruct((tokens, D_MODEL), F32),
        grid=(tokens // tm,),
        in_specs=[tok, _resident((1, D_MODEL)), _resident((D_MODEL, D_FF)), _resident((D_FF, D_MODEL)),
                  _resident((1, D_MODEL))],
        out_specs=tok,
        compiler_params=_params(("parallel",)),
        name="mlp_norm",
    )(h, nw, w1, w2, nf)


def _rope_tables(seq_len):
    half = HEAD_DIM // 2
    inv = ROPE_THETA ** (-(jnp.arange(half, dtype=F32) * 2.0 / HEAD_DIM))
    ang = jnp.arange(seq_len, dtype=F32)[:, None] * inv[None, :]
    cos, sin = jnp.cos(ang), jnp.sin(ang)
    reps = LANES // HEAD_DIM
    return (jnp.tile(jnp.concatenate([cos, cos], axis=-1), (1, reps)),
            jnp.tile(jnp.concatenate([-sin, sin], axis=-1), (1, reps)))


def _trunk(x, p):
    batch, seq_len, _ = x.shape
    x2 = x.reshape(batch * seq_len, D_MODEL)
    cos, sin = _rope_tables(seq_len)
    qkv = _qkv_call(x2, p["norm_mix"], p["w_qkv"], cos, sin, seq_len)
    z, xbc, dt = _ssmproj_call(x2, p["norm_mix"], p["w_z"], p["w_xbc"], p["w_dt_t"], p["dt_bias"], batch, seq_len)
    outs, lses = [], []
    groups = len(DILATIONS)
    for group, dilation in enumerate(DILATIONS):
        o, lse = _attn_call(qkv[group], qkv[groups + group], qkv[2 * groups + group], dilation, batch, seq_len)
        outs.append(o)
        lses.append(lse)
    y = _ssd_call(xbc.reshape(batch, seq_len, CONV_DIM), dt, p["conv_w"], p["conv_b"], p["a_log"], p["d_skip"],
                  batch, seq_len)
    h = _merge_call(x2, outs, lses, y.reshape(batch * seq_len, D_INNER), z, p["norm_mix"], p["ssm_norm"],
                    p["w_ga"], p["w_gm"], p["w_ab"], p["w_sb"], p["w_out"], p["expand"])
    out = _mlp_call(h, p["norm_mlp"], p["w_mlp_in"], p["w_mlp_out"], p["norm_final"])
    return out.reshape(batch, seq_len, D_MODEL)


def _prepare(norm_mix, w_in, conv_w, conv_b, dt_bias, a_log, d_skip, ssm_norm, w_attn_branch, w_ssm_branch,
             w_out, norm_mlp, w_mlp_in, w_mlp_out, norm_final):
    w = w_in[0]
    bounds = {}
    start = 0
    for name, width in (("qkv", 3 * QKV_WIDTH), ("z", D_INNER), ("xbc", CONV_DIM), ("dt", 2 * SSM_HEADS),
                        ("ga", D_MODEL), ("gm", D_MODEL)):
        bounds[name] = (start, start + width)
        start += width
    cut = lambda name: w[:, bounds[name][0]:bounds[name][1]]
    perm = jnp.arange(2 * SSM_HEADS).reshape(2, SSM_GROUPS, HEADS_PER_SSM_GROUP).transpose(1, 0, 2).reshape(-1)
    heads_lanes = jnp.arange(LANES)[:, None] == (jnp.arange(GROUP_WIDTH)[None, :] // HEAD_DIM)
    return {
        "norm_mix": norm_mix[0][None, :],
        "w_qkv": cut("qkv").astype(BF16),
        "w_z": cut("z").astype(BF16),
        "w_xbc": cut("xbc").astype(BF16),
        "w_dt_t": cut("dt")[:, perm].T,
        "dt_bias": dt_bias[0].reshape(-1)[perm][:, None],
        "a_log": a_log[0].reshape(-1)[perm].reshape(SSM_GROUPS, 2 * HEADS_PER_SSM_GROUP, 1),
        "d_skip": jnp.repeat(d_skip[0].reshape(SSM_GROUPS, HEADS_PER_SSM_GROUP), HEAD_DIM, axis=1)[:, None, :],
        "conv_w": conv_w[0],
        "conv_b": conv_b[0][None, :],
        "ssm_norm": ssm_norm[0][None, :],
        "w_ga": cut("ga").astype(BF16),
        "w_gm": cut("gm").astype(BF16),
        "w_ab": w_attn_branch[0].astype(BF16),
        "w_sb": w_ssm_branch[0].astype(BF16),
        "w_out": w_out[0].astype(BF16),
        "expand": heads_lanes.astype(BF16),
        "norm_mlp": norm_mlp[0][None, :],
        "w_mlp_in": w_mlp_in[0].astype(BF16),
        "w_mlp_out": w_mlp_out[0].astype(BF16),
        "norm_final": norm_final[None, :],
    }


def kernel(x_prompt, x_sample, norm_mix, w_in, conv_w, conv_b, dt_bias, a_log, d_skip, ssm_norm, w_attn_branch,
           w_ssm_branch, w_out, norm_mlp, w_mlp_in, w_mlp_out, norm_final):
    p = _prepare(norm_mix, w_in, conv_w, conv_b, dt_bias, a_log, d_skip, ssm_norm, w_attn_branch, w_ssm_branch,
                 w_out, norm_mlp, w_mlp_in, w_mlp_out, norm_final)
    return _trunk(x_prompt, p), _trunk(x_sample, p)
```

```python
import functools
import math

import jax
import jax.numpy as jnp
from jax import lax
from jax.experimental import pallas as pl
from jax.experimental.pallas import tpu as pltpu

F32 = jnp.float32
BF16 = jnp.bfloat16

D_MODEL = 1024
HEAD_DIM = 64
DILATIONS = (1, 4, 16)
RADIUS = 64
HEADS_PER_GROUP = 8
GROUP_WIDTH = HEADS_PER_GROUP * HEAD_DIM
QKV_WIDTH = len(DILATIONS) * GROUP_WIDTH
ROPE_THETA = 10000.0
D_INNER = 2048
SSM_GROUPS = 8
HEADS_PER_SSM_GROUP = 4
SSM_HEADS = SSM_GROUPS * HEADS_PER_SSM_GROUP
SSM_GROUP_WIDTH = HEADS_PER_SSM_GROUP * HEAD_DIM
D_STATE = 128
CONV_WIDTH = 5
CONV_DIM = D_INNER + 2 * SSM_GROUPS * D_STATE
CHUNK = 128
D_FF = 4096
RMS_EPS = 1e-6
NEG_INF = -1e30

LANES = 128
CONV_HALO = 64
CONV_ROWS = 32
LOG2_E = math.log2(math.e)
VMEM_LIMIT_BYTES = 56 * 1024 * 1024

QKV_TOKENS = 1024
SSMPROJ_TOKENS = 1024
MERGE_TOKENS = 512
MLP_TOKENS = 1024
ATTN_QUERY_TILE = {1: 512, 4: 256, 16: 128}
ATTN_QUERY_BLOCK = 128
MATMUL_COLS = 512


def _params(semantics):
    return pltpu.CompilerParams(dimension_semantics=semantics, vmem_limit_bytes=VMEM_LIMIT_BYTES)


def _resident(shape):
    zeros = (0,) * len(shape)
    return pl.BlockSpec(shape, lambda *_: zeros, pipeline_mode=pl.Buffered(1))


def _rms_scale(x, w):
    return x * lax.rsqrt(jnp.mean(x * x, axis=-1, keepdims=True) + RMS_EPS) * w


def _dot(a, b):
    return jnp.dot(a, b, preferred_element_type=F32)


def _dot_nt(a, b):
    return lax.dot_general(a, b, (((1,), (1,)), ((), ())), preferred_element_type=F32)


def _qkv_kernel(x_ref, nw_ref, w_ref, cos_ref, sin_ref, *refs):
    out_refs, stage = refs[:-1], refs[-1]
    tm = x_ref.shape[0]
    u = _rms_scale(x_ref[...], nw_ref[...]).astype(BF16)
    cos = cos_ref[...]
    sin = sin_ref[...]
    lane = lax.broadcasted_iota(jnp.int32, cos.shape, 1)
    first_half = (lane % HEAD_DIM) < (HEAD_DIM // 2)

    def rope(p):
        partner = jnp.where(first_half, pltpu.roll(p, LANES - HEAD_DIM // 2, axis=1),
                            pltpu.roll(p, HEAD_DIM // 2, axis=1))
        return p * cos + partner * sin

    for seg, (rotary, scale) in enumerate(((True, LOG2_E / math.sqrt(HEAD_DIM)), (True, 1.0), (False, 1.0))):
        for group, dilation in enumerate(DILATIONS):
            o_ref = out_refs[seg * len(DILATIONS) + group]
            col = seg * QKV_WIDTH + group * GROUP_WIDTH
            acc = _dot(u, w_ref[:, col:col + GROUP_WIDTH])
            for t in range(GROUP_WIDTH // LANES):
                piece = acc[:, t * LANES:(t + 1) * LANES]
                if rotary:
                    piece = rope(piece) * scale
                if dilation == 1:
                    o_ref[:, t * LANES:(t + 1) * LANES] = piece.astype(BF16)
                else:
                    stage[t] = piece
                    for r in range(dilation):
                        o_ref[:, r * GROUP_WIDTH + t * LANES:r * GROUP_WIDTH + (t + 1) * LANES] = (
                            stage[t, pl.ds(r, tm // dilation, stride=dilation), :].astype(BF16))


def _qkv_call(x2, nw, w_qkv, cos, sin, seq_len):
    tokens = x2.shape[0]
    tm = QKV_TOKENS
    pos_blocks = seq_len // tm
    tok = lambda i: (i, 0)
    pos = lambda i: (i % pos_blocks, 0)
    shapes, specs = [], []
    for _ in range(3):
        for d in DILATIONS:
            shapes.append(jax.ShapeDtypeStruct((tokens // d, d * GROUP_WIDTH), BF16))
            specs.append(pl.BlockSpec((tm // d, d * GROUP_WIDTH), tok))
    return pl.pallas_call(
        _qkv_kernel,
        out_shape=tuple(shapes),
        grid=(tokens // tm,),
        in_specs=[pl.BlockSpec((tm, D_MODEL), tok), _resident((1, D_MODEL)),
                  _resident((D_MODEL, 3 * QKV_WIDTH)),
                  pl.BlockSpec((tm, LANES), pos), pl.BlockSpec((tm, LANES), pos)],
        out_specs=tuple(specs),
        scratch_shapes=[pltpu.VMEM((GROUP_WIDTH // LANES, tm, LANES), F32)],
        compiler_params=_params(("parallel",)),
        name="qkv_proj",
    )(x2, nw, w_qkv, cos, sin)


def _softplus(x):
    return jnp.maximum(x, 0.0) + jnp.log1p(jnp.exp(-jnp.abs(x)))


def _ssmproj_kernel(x_ref, nw_ref, wz_ref, wx_ref, wdt_ref, dtb_ref, z_ref, xbc_ref, dt_ref):
    u32 = _rms_scale(x_ref[...], nw_ref[...])
    u = u32.astype(BF16)
    for c in range(D_INNER // MATMUL_COLS):
        sl = slice(c * MATMUL_COLS, (c + 1) * MATMUL_COLS)
        z_ref[:, sl] = _dot(u, wz_ref[:, sl]).astype(BF16)
    for c in range(CONV_DIM // MATMUL_COLS):
        sl = slice(c * MATMUL_COLS, (c + 1) * MATMUL_COLS)
        xbc_ref[:, sl] = _dot(u, wx_ref[:, sl]).astype(BF16)
    dt_t = lax.dot_general(wdt_ref[...], u32, (((1,), (1,)), ((), ())),
                           precision=lax.Precision.HIGHEST, preferred_element_type=F32)
    dt_t = _softplus(dt_t + dtb_ref[...])
    rows = 2 * HEADS_PER_SSM_GROUP
    for g in range(SSM_GROUPS):
        for c in range(dt_ref.shape[1] // rows):
            dt_ref[g, c * rows:(c + 1) * rows, :] = dt_t[g * rows:(g + 1) * rows, c * CHUNK:(c + 1) * CHUNK]


def _ssmproj_call(x2, nw, w_z, w_xbc, w_dt_t, dt_bias, batch, seq_len):
    tokens = x2.shape[0]
    tm = SSMPROJ_TOKENS
    per_seq = seq_len // tm
    rows = 2 * HEADS_PER_SSM_GROUP
    tok = lambda i: (i, 0)
    return pl.pallas_call(
        _ssmproj_kernel,
        out_shape=(jax.ShapeDtypeStruct((tokens, D_INNER), BF16),
                   jax.ShapeDtypeStruct((tokens, CONV_DIM), BF16),
                   jax.ShapeDtypeStruct((SSM_GROUPS, batch, seq_len // CHUNK * rows, CHUNK), F32)),
        grid=(tokens // tm,),
        in_specs=[pl.BlockSpec((tm, D_MODEL), tok), _resident((1, D_MODEL)),
                  _resident((D_MODEL, D_INNER)), _resident((D_MODEL, CONV_DIM)),
                  _resident((2 * SSM_HEADS, D_MODEL)), _resident((2 * SSM_HEADS, 1))],
        out_specs=(pl.BlockSpec((tm, D_INNER), tok), pl.BlockSpec((tm, CONV_DIM), tok),
                   pl.BlockSpec((SSM_GROUPS, None, tm // CHUNK * rows, CHUNK),
                                lambda i: (0, i // per_seq, i % per_seq, 0))),
        compiler_params=_params(("parallel",)),
        name="ssm_proj",
    )(x2, nw, w_z, w_xbc, w_dt_t, dt_bias)


def _attn_kernel(q_ref, kp_ref, km_ref, kn_ref, vp_ref, vm_ref, vn_ref, o_ref, lse_ref, o_s, k_s, v_s, *, tq,
                 sub_len, dilation):
    j = pl.program_id(1)
    r = pl.program_id(2)
    for dst, parts in ((k_s, (kp_ref, km_ref, kn_ref)), (v_s, (vp_ref, vm_ref, vn_ref))):
        dst[0:RADIUS] = parts[0][...]
        dst[RADIUS:RADIUS + tq] = parts[1][...]
        dst[RADIUS + tq:] = parts[2][...]
    qb = min(ATTN_QUERY_BLOCK, tq)
    kb = qb + 2 * RADIUS
    row = lax.broadcasted_iota(jnp.int32, (qb, kb), 0)
    col = lax.broadcasted_iota(jnp.int32, (qb, kb), 1)
    band = (col >= row) & (col <= row + 2 * RADIUS)
    head_lane = lax.broadcasted_iota(jnp.int32, (qb, LANES), 1)
    low_q = lax.broadcasted_iota(jnp.int32, (qb, LANES), 1) < HEAD_DIM

    def query_block(sb, carry):
        base = pl.multiple_of(sb * qb, qb)
        key_pos = j * tq + base - RADIUS + col
        valid = band & (key_pos >= 0) & (key_pos < sub_len)
        rows = pl.ds(r + base * dilation, qb, stride=dilation) if dilation > 1 else pl.ds(base, qb)
        lse_tile = jnp.zeros((qb, LANES), F32)
        for pair in range(HEADS_PER_GROUP // 2):
            sl = slice(pair * LANES, (pair + 1) * LANES)
            qp = q_ref[pl.ds(base, qb), sl]
            kp = k_s[pl.ds(base, kb), sl]
            vp = v_s[pl.ds(base, kb), sl]
            halves = []
            for half in range(2):
                keep_q = low_q if half == 0 else ~low_q
                s = jnp.where(valid, _dot_nt(jnp.where(keep_q, qp, 0), kp), NEG_INF)
                m = jnp.max(s, axis=-1, keepdims=True)
                p = jnp.exp2(s - m)
                l = jnp.sum(p, axis=-1, keepdims=True)
                halves.append(_dot(p.astype(BF16), vp) / l)
                lse_tile = jnp.where(head_lane == 2 * pair + half, (m + jnp.log2(l)) * (1.0 / LOG2_E), lse_tile)
            o_s[pair, rows, :] = jnp.where(low_q, halves[0], halves[1])
        lse_ref[rows, :] = lse_tile
        return carry

    lax.fori_loop(0, tq // qb, query_block, 0, unroll=True)

    @pl.when(r == dilation - 1)
    def _():
        for pair in range(HEADS_PER_GROUP // 2):
            o_ref[:, pair * LANES:(pair + 1) * LANES] = o_s[pair].astype(BF16)


def _attn_call(q, k, v, dilation, batch, seq_len):
    sub_len = seq_len // dilation
    tq = min(ATTN_QUERY_TILE[dilation], sub_len)
    halo_per_tile = tq // RADIUS
    last_halo = sub_len // RADIUS - 1
    view = lambda a: a.reshape(batch, sub_len, dilation * GROUP_WIDTH)
    main = lambda b, j, r: (b, j, r)
    prev = lambda b, j, r: (b, jnp.maximum(j * halo_per_tile - 1, 0), r)
    nxt = lambda b, j, r: (b, jnp.minimum((j + 1) * halo_per_tile, last_halo), r)
    out_map = lambda b, j, r: (b, j, 0)
    tile = lambda m: pl.BlockSpec((None, tq, GROUP_WIDTH), m)
    halo = lambda m: pl.BlockSpec((None, RADIUS, GROUP_WIDTH), m)
    qv, kv, vv = view(q), view(k), view(v)
    out, lse = pl.pallas_call(
        functools.partial(_attn_kernel, tq=tq, sub_len=sub_len, dilation=dilation),
        out_shape=(jax.ShapeDtypeStruct((batch, seq_len, GROUP_WIDTH), BF16),
                   jax.ShapeDtypeStruct((batch, seq_len, LANES), F32)),
        grid=(batch, sub_len // tq, dilation),
        in_specs=[tile(main), halo(prev), tile(main), halo(nxt), halo(prev), tile(main), halo(nxt)],
        out_specs=(pl.BlockSpec((None, tq * dilation, GROUP_WIDTH), out_map),
                   pl.BlockSpec((None, tq * dilation, LANES), out_map)),
        scratch_shapes=[pltpu.VMEM((GROUP_WIDTH // LANES, tq * dilation, LANES), F32),
                        pltpu.VMEM((tq + 2 * RADIUS, GROUP_WIDTH), BF16),
                        pltpu.VMEM((tq + 2 * RADIUS, GROUP_WIDTH), BF16)],
        compiler_params=_params(("parallel", "parallel", "arbitrary")),
        name=f"band_attn_d{dilation}",
    )(qv, kv, kv, kv, vv, vv, vv)
    return out.reshape(batch * seq_len, GROUP_WIDTH), lse.reshape(batch * seq_len, LANES)


def _ssd_kernel(xp_ref, bp_ref, cp_ref, dt_ref, cwx_ref, cwb_ref, cwc_ref, cbx_ref, cbb_ref, cbc_ref,
                alog_ref, dsk_ref, y_ref, xs_s, bm_s, cm_s, bt_s, hb_s, hf_s, hbc_s, acs_s, inj_s, dec_s, cols_s, shift_s, *,
                seq_len):
    L = CHUNK
    nc = seq_len // L
    heads = HEADS_PER_SSM_GROUP
    pair_w = 2 * HEAD_DIM

    side_taps = [t for t in range(CONV_WIDTH) if t != CONV_WIDTH // 2]
    sel_row = lax.broadcasted_iota(jnp.int32, (len(side_taps) * L, 2 * L), 0)
    sel_col = lax.broadcasted_iota(jnp.int32, (len(side_taps) * L, 2 * L), 1)
    slab = len(side_taps) * CONV_ROWS
    tap_slot = sel_row % slab // CONV_ROWS
    tap_offset = jnp.where(tap_slot < CONV_WIDTH // 2, tap_slot, tap_slot + 1) - CONV_WIDTH // 2
    out_row = sel_row // slab * CONV_ROWS + sel_row % CONV_ROWS
    shift_s[...] = (sel_col == CONV_HALO + out_row + tap_offset).astype(BF16)

    def conv_chunk(c, carry):
        r0 = pl.multiple_of(c * L, L)
        lo = pl.multiple_of(jnp.maximum(r0 - CONV_HALO, 0), CONV_HALO)
        hi = pl.multiple_of(jnp.minimum(r0 + L, seq_len - CONV_HALO), CONV_HALO)
        shifts = shift_s[...]

        def window(rows, size, keep):
            parts = [src[pl.ds(rows, size), :] for src in (xp_ref, bp_ref, cp_ref)]
            parts = [jnp.where(keep, part, jnp.zeros_like(part)) for part in parts]
            return parts[0], jnp.concatenate(parts[1:], axis=1)

        windows = (window(lo, CONV_HALO, c > 0), window(r0, L, True), window(hi, CONV_HALO, c < nc - 1))
        weights = (cwx_ref[...], jnp.concatenate([cwb_ref[...], cwc_ref[...]], axis=1))
        biases = (cbx_ref[...], jnp.concatenate([cbb_ref[...], cbc_ref[...]], axis=1))
        acts = []
        for half in range(2):
            before, cur, after = (windows[part][half] for part in range(3))
            shifted = _dot(shifts, jnp.concatenate([before, cur, after], axis=0))
            w = weights[half]
            blocks = []
            for blk in range(L // CONV_ROWS):
                rows = slice(blk * CONV_ROWS, (blk + 1) * CONV_ROWS)
                acc = biases[half] + cur[rows].astype(F32) * w[CONV_WIDTH // 2:CONV_WIDTH // 2 + 1]
                for slot, tap in enumerate(side_taps):
                    start = (blk * len(side_taps) + slot) * CONV_ROWS
                    acc = acc + shifted[start:start + CONV_ROWS] * w[tap:tap + 1]
                blocks.append(acc * jax.nn.sigmoid(acc))
            acts.append(jnp.concatenate(blocks, axis=0))
        xs_s[pl.ds(r0, L), :] = acts[0].astype(BF16)
        bm_s[pl.ds(r0, L), :] = acts[1][:, :D_STATE].astype(BF16)
        cm_s[pl.ds(r0, L), :] = acts[1][:, D_STATE:].astype(BF16)
        bt_s[c] = jnp.transpose(acts[1][:, :D_STATE]).astype(BF16)
        return carry

    rows8 = 2 * heads
    dt_all = dt_ref[...]
    a_rows = -jnp.exp(alog_ref[...]) * LOG2_E
    dta = dt_all * jnp.concatenate([a_rows] * nc, axis=0)
    fwd_rows = lax.broadcasted_iota(jnp.int32, dta.shape, 0) % rows8 < heads
    ones_upper = (lax.broadcasted_iota(jnp.int32, (L, L), 0) <= lax.broadcasted_iota(jnp.int32, (L, L), 1)).astype(BF16)
    cs = jnp.zeros(dta.shape, F32)
    rest = dta
    for _ in range(3):
        piece = rest.astype(BF16)
        cs = cs + _dot(piece, ones_upper)
        rest = rest - piece.astype(F32)
    total = cs[:, L - 1:L]
    acs_all = jnp.where(fwd_rows, cs, total - cs + dta)
    acs_s[...] = acs_all
    inj_s[...] = dt_all * jnp.exp2(total - acs_all)
    dec_s[...] = jnp.broadcast_to(jnp.exp2(total), dta.shape)
    pad_rows = jnp.zeros((L - rows8, L), F32)

    def column_chunk(c, carry):
        acs = acs_s[pl.ds(pl.multiple_of(c * rows8, rows8), rows8), :]
        cols_s[c] = jnp.transpose(jnp.concatenate([acs, pad_rows], axis=0))
        return carry

    def prepass(c, carry):
        return column_chunk(c, conv_chunk(c, carry))

    lax.fori_loop(0, nc, prepass, 0, unroll=8)

    hf_s[...] = jnp.zeros_like(hf_s)
    hbc_s[...] = jnp.zeros_like(hbc_s)
    dskip = dsk_ref[...]
    row = lax.broadcasted_iota(jnp.int32, (L, L), 0)
    col = lax.broadcasted_iota(jnp.int32, (L, L), 1)
    lower, strict_lower, strict_upper = row >= col, row > col, row < col
    low_lanes = lax.broadcasted_iota(jnp.int32, (L, pair_w), 1) < HEAD_DIM
    state_lane = lax.broadcasted_iota(jnp.int32, (1, pair_w), 1) < HEAD_DIM

    def decays(c):
        rows = pl.ds(pl.multiple_of(c * rows8, rows8), rows8)
        return dt_ref[rows, :], acs_s[rows, :], dec_s[rows, 0:1], inj_s[rows, :]

    def split_heads(xp):
        return jnp.concatenate([jnp.where(low_lanes, xp, 0), jnp.where(low_lanes, 0, xp)], axis=0)

    def state_update(h_s, c, r0, chunk_decay, inject, base):
        bt = bt_s[c].astype(F32)
        for pr in range(heads // 2):
            sl = slice(pr * pair_w, (pr + 1) * pair_w)
            hr = base + 2 * pr
            lhs = jnp.concatenate([(bt * inject[hr:hr + 1]).astype(BF16),
                                   (bt * inject[hr + 1:hr + 2]).astype(BF16)], axis=1)
            st = _dot(lhs, split_heads(xs_s[pl.ds(r0, L), sl]))
            dec = jnp.where(state_lane, chunk_decay[hr:hr + 1], chunk_decay[hr + 1:hr + 2])
            h_s[:, sl] = h_s[:, sl] * dec + st

    def sweep_a(i, carry):
        c = nc - 1 - i
        r0 = pl.multiple_of(c * L, L)
        _, _, chunk_decay, inject = decays(c)
        hb_s[c] = hbc_s[...].astype(BF16)
        state_update(hbc_s, c, r0, chunk_decay, inject, heads)
        return carry

    def sweep_b(c, carry):
        r0 = pl.multiple_of(c * L, L)
        dt8, acs, chunk_decay, inject = decays(c)
        cols = cols_s[c]
        cm = cm_s[pl.ds(r0, L), :]
        cb = _dot_nt(cm, bm_s[pl.ds(r0, L), :])
        from_fwd = _dot(cm, hf_s[...].astype(BF16))
        from_bwd = _dot(cm, hb_s[c])
        for pr in range(heads // 2):
            sl = slice(pr * pair_w, (pr + 1) * pair_w)
            w_sum, col_f, col_b = [], [], []
            for e in (2 * pr, 2 * pr + 1):
                fwd, bwd = slice(e, e + 1), slice(heads + e, heads + e + 1)
                col_f.append(jnp.broadcast_to(cols[:, fwd], (L, L)))
                col_b.append(jnp.broadcast_to(cols[:, bwd], (L, L)))
                decay = jnp.exp2(jnp.where(lower, col_f[-1] - acs[fwd], col_b[-1] - acs[bwd]))
                dt_sel = jnp.where(strict_lower, dt8[fwd], jnp.where(strict_upper, dt8[bwd], dt8[fwd] + dt8[bwd]))
                w_sum.append((cb * (decay * dt_sel)).astype(BF16))
            xp = xs_s[pl.ds(r0, L), sl]
            y = _dot(jnp.concatenate(w_sum, axis=1), split_heads(xp))
            y = y + from_fwd[:, sl] * jnp.exp2(jnp.where(low_lanes, col_f[0], col_f[1]))
            y = y + from_bwd[:, sl] * jnp.exp2(jnp.where(low_lanes, col_b[0], col_b[1]))
            y = y + xp.astype(F32) * dskip[:, sl]
            y_ref[pl.ds(r0, L), sl] = y.astype(BF16)
        state_update(hf_s, c, r0, chunk_decay, inject, 0)
        return carry

    lax.fori_loop(0, nc, sweep_a, 0, unroll=4)
    lax.fori_loop(0, nc, sweep_b, 0, unroll=8)


def _ssd_call(xbc, dt, conv_w, conv_b, a_log, d_skip, batch, seq_len):
    b_off = D_INNER // D_STATE
    c_off = b_off + SSM_GROUPS
    nc = seq_len // CHUNK
    seq_block = lambda w, m: pl.BlockSpec((None, seq_len, w), m)
    rows = 2 * HEADS_PER_SSM_GROUP
    return pl.pallas_call(
        functools.partial(_ssd_kernel, seq_len=seq_len),
        out_shape=jax.ShapeDtypeStruct((batch, seq_len, D_INNER), BF16),
        grid=(batch, SSM_GROUPS),
        in_specs=[seq_block(SSM_GROUP_WIDTH, lambda b, g: (b, 0, g)),
                  seq_block(D_STATE, lambda b, g: (b, 0, b_off + g)),
                  seq_block(D_STATE, lambda b, g: (b, 0, c_off + g)),
                  pl.BlockSpec((None, None, nc * rows, CHUNK), lambda b, g: (g, b, 0, 0)),
                  pl.BlockSpec((CONV_WIDTH, SSM_GROUP_WIDTH), lambda b, g: (0, g)),
                  pl.BlockSpec((CONV_WIDTH, D_STATE), lambda b, g: (0, b_off + g)),
                  pl.BlockSpec((CONV_WIDTH, D_STATE), lambda b, g: (0, c_off + g)),
                  pl.BlockSpec((1, SSM_GROUP_WIDTH), lambda b, g: (0, g)),
                  pl.BlockSpec((1, D_STATE), lambda b, g: (0, b_off + g)),
                  pl.BlockSpec((1, D_STATE), lambda b, g: (0, c_off + g)),
                  pl.BlockSpec((None, rows, 1), lambda b, g: (g, 0, 0)),
                  pl.BlockSpec((None, 1, SSM_GROUP_WIDTH), lambda b, g: (g, 0, 0))],
        out_specs=seq_block(SSM_GROUP_WIDTH, lambda b, g: (b, 0, g)),
        scratch_shapes=[pltpu.VMEM((seq_len, SSM_GROUP_WIDTH), BF16), pltpu.VMEM((seq_len, D_STATE), BF16),
                        pltpu.VMEM((seq_len, D_STATE), BF16), pltpu.VMEM((nc, D_STATE, CHUNK), BF16),
                        pltpu.VMEM((nc, D_STATE, SSM_GROUP_WIDTH), BF16),
                        pltpu.VMEM((D_STATE, SSM_GROUP_WIDTH), F32), pltpu.VMEM((D_STATE, SSM_GROUP_WIDTH), F32),
                        pltpu.VMEM((nc * rows, CHUNK), F32), pltpu.VMEM((nc * rows, CHUNK), F32),
                        pltpu.VMEM((nc * rows, CHUNK), F32), pltpu.VMEM((nc, CHUNK, LANES), F32),
                        pltpu.VMEM(((CONV_WIDTH - 1) * CHUNK, 2 * CHUNK), BF16)],
        compiler_params=_params(("parallel", "parallel")),
        name="conv_ssd",
    )(xbc, xbc, xbc, dt, conv_w, conv_w, conv_w, conv_b, conv_b, conv_b, a_log, d_skip)


def _merge_kernel(x_ref, o0_ref, o1_ref, o2_ref, l0_ref, l1_ref, l2_ref, y_ref, z_ref, nw_ref, sn_ref,
                  wga_ref, wgm_ref, wab_ref, wsb_ref, wo_ref, ex_ref, h_ref):
    x = x_ref[...]
    u = _rms_scale(x, nw_ref[...]).astype(BF16)
    lses = (l0_ref[...], l1_ref[...], l2_ref[...])
    top = jnp.maximum(jnp.maximum(lses[0], lses[1]), lses[2])
    es = [jnp.exp(l - top) for l in lses]
    inv = 1.0 / (es[0] + es[1] + es[2])
    expand = ex_ref[...]
    mix = jnp.zeros((x.shape[0], GROUP_WIDTH), F32)
    for e, o_ref in zip(es, (o0_ref, o1_ref, o2_ref)):
        alpha = e * inv
        hi = alpha.astype(BF16)
        lo = (alpha - hi.astype(F32)).astype(BF16)
        mix = mix + (_dot(hi, expand) + _dot(lo, expand)) * o_ref[...].astype(F32)
    o_attn = _dot(mix.astype(BF16), wab_ref[...])
    z = z_ref[...].astype(F32)
    yz = y_ref[...].astype(F32) * (z * jax.nn.sigmoid(z))
    o_ssm = _dot(_rms_scale(yz, sn_ref[...]).astype(BF16), wsb_ref[...])
    merged = jax.nn.sigmoid(_dot(u, wga_ref[...])) * o_attn + jax.nn.sigmoid(_dot(u, wgm_ref[...])) * o_ssm
    h_ref[...] = x + _dot(merged.astype(BF16), wo_ref[...])


def _merge_call(x2, outs, lses, y, z, nw, ssm_norm, w_ga, w_gm, w_ab, w_sb, w_out, expand):
    tokens = x2.shape[0]
    tm = MERGE_TOKENS
    tok = lambda w: pl.BlockSpec((tm, w), lambda i: (i, 0))
    return pl.pallas_call(
        _merge_kernel,
        out_shape=jax.ShapeDtypeStruct((tokens, D_MODEL), F32),
        grid=(tokens // tm,),
        in_specs=[tok(D_MODEL), tok(GROUP_WIDTH), tok(GROUP_WIDTH), tok(GROUP_WIDTH),
                  tok(LANES), tok(LANES), tok(LANES), tok(D_INNER), tok(D_INNER),
                  _resident((1, D_MODEL)), _resident((1, D_INNER)),
                  _resident((D_MODEL, D_MODEL)), _resident((D_MODEL, D_MODEL)),
                  _resident((GROUP_WIDTH, D_MODEL)), _resident((D_INNER, D_MODEL)),
                  _resident((D_MODEL, D_MODEL)), _resident((LANES, GROUP_WIDTH))],
        out_specs=tok(D_MODEL),
        compiler_params=_params(("parallel",)),
        name="merge_out",
    )(x2, *outs, *lses, y, z, nw, ssm_norm, w_ga, w_gm, w_ab, w_sb, w_out, expand)


def _mlp_kernel(h_ref, nw_ref, w1_ref, w2_ref, nf_ref, o_ref):
    h = h_ref[...]
    u = _rms_scale(h, nw_ref[...]).astype(BF16)
    acc = h
    for c in range(D_FF // MATMUL_COLS):
        sl = slice(c * MATMUL_COLS, (c + 1) * MATMUL_COLS)
        a = jnp.maximum(_dot(u, w1_ref[:, sl]), 0.0)
        acc = acc + _dot((a * a).astype(BF16), w2_ref[sl, :])
    o_ref[...] = _rms_scale(acc, nf_ref[...])


def _mlp_call(h, nw, w1, w2, nf):
    tokens = h.shape[0]
    tm = MLP_TOKENS
    tok = pl.BlockSpec((tm, D_MODEL), lambda i: (i, 0))
    return pl.pallas_call(
        _mlp_kernel,
        out_shape=jax.ShapeDtypeStruct((tokens, D_MODEL), F32),
        grid=(tokens // tm,),
        in_specs=[tok, _resident((1, D_MODEL)), _resident((D_MODEL, D_FF)), _resident((D_FF, D_MODEL)),
                  _resident((1, D_MODEL))],
        out_specs=tok,
        compiler_params=_params(("parallel",)),
        name="mlp_norm",
    )(h, nw, w1, w2, nf)


def _rope_tables(seq_len):
    half = HEAD_DIM // 2
    inv = ROPE_THETA ** (-(jnp.arange(half, dtype=F32) * 2.0 / HEAD_DIM))
    ang = jnp.arange(seq_len, dtype=F32)[:, None] * inv[None, :]
    cos, sin = jnp.cos(ang), jnp.sin(ang)
    reps = LANES // HEAD_DIM
    return (jnp.tile(jnp.concatenate([cos, cos], axis=-1), (1, reps)),
            jnp.tile(jnp.concatenate([-sin, sin], axis=-1), (1, reps)))


def _trunk(x, p):
    batch, seq_len, _ = x.shape
    x2 = x.reshape(batch * seq_len, D_MODEL)
    cos, sin = _rope_tables(seq_len)
    qkv = _qkv_call(x2, p["norm_mix"], p["w_qkv"], cos, sin, seq_len)
    z, xbc, dt = _ssmproj_call(x2, p["norm_mix"], p["w_z"], p["w_xbc"], p["w_dt_t"], p["dt_bias"], batch, seq_len)
    outs, lses = [], []
    groups = len(DILATIONS)
    for group, dilation in enumerate(DILATIONS):
        o, lse = _attn_call(qkv[group], qkv[groups + group], qkv[2 * groups + group], dilation, batch, seq_len)
        outs.append(o)
        lses.append(lse)
    y = _ssd_call(xbc.reshape(batch, seq_len, CONV_DIM), dt, p["conv_w"], p["conv_b"], p["a_log"], p["d_skip"],
                  batch, seq_len)
    h = _merge_call(x2, outs, lses, y.reshape(batch * seq_len, D_INNER), z, p["norm_mix"], p["ssm_norm"],
                    p["w_ga"], p["w_gm"], p["w_ab"], p["w_sb"], p["w_out"], p["expand"])
    out = _mlp_call(h, p["norm_mlp"], p["w_mlp_in"], p["w_mlp_out"], p["norm_final"])
    return out.reshape(batch, seq_len, D_MODEL)


def _prepare(norm_mix, w_in, conv_w, conv_b, dt_bias, a_log, d_skip, ssm_norm, w_attn_branch, w_ssm_branch,
             w_out, norm_mlp, w_mlp_in, w_mlp_out, norm_final):
    w = w_in[0]
    bounds = {}
    start = 0
    for name, width in (("qkv", 3 * QKV_WIDTH), ("z", D_INNER), ("xbc", CONV_DIM), ("dt", 2 * SSM_HEADS),
                        ("ga", D_MODEL), ("gm", D_MODEL)):
        bounds[name] = (start, start + width)
        start += width
    cut = lambda name: w[:, bounds[name][0]:bounds[name][1]]
    perm = jnp.arange(2 * SSM_HEADS).reshape(2, SSM_GROUPS, HEADS_PER_SSM_GROUP).transpose(1, 0, 2).reshape(-1)
    heads_lanes = jnp.arange(LANES)[:, None] == (jnp.arange(GROUP_WIDTH)[None, :] // HEAD_DIM)
    return {
        "norm_mix": norm_mix[0][None, :],
        "w_qkv": cut("qkv").astype(BF16),
        "w_z": cut("z").astype(BF16),
        "w_xbc": cut("xbc").astype(BF16),
        "w_dt_t": cut("dt")[:, perm].T,
        "dt_bias": dt_bias[0].reshape(-1)[perm][:, None],
        "a_log": a_log[0].reshape(-1)[perm].reshape(SSM_GROUPS, 2 * HEADS_PER_SSM_GROUP, 1),
        "d_skip": jnp.repeat(d_skip[0].reshape(SSM_GROUPS, HEADS_PER_SSM_GROUP), HEAD_DIM, axis=1)[:, None, :],
        "conv_w": conv_w[0],
        "conv_b": conv_b[0][None, :],
        "ssm_norm": ssm_norm[0][None, :],
        "w_ga": cut("ga").astype(BF16),
        "w_gm": cut("gm").astype(BF16),
        "w_ab": w_attn_branch[0].astype(BF16),
        "w_sb": w_ssm_branch[0].astype(BF16),
        "w_out": w_out[0].astype(BF16),
        "expand": heads_lanes.astype(BF16),
        "norm_mlp": norm_mlp[0][None, :],
        "w_mlp_in": w_mlp_in[0].astype(BF16),
        "w_mlp_out": w_mlp_out[0].astype(BF16),
        "norm_final": norm_final[None, :],
    }


def kernel(x_prompt, x_sample, norm_mix, w_in, conv_w, conv_b, dt_bias, a_log, d_skip, ssm_norm, w_attn_branch,
           w_ssm_branch, w_out, norm_mlp, w_mlp_in, w_mlp_out, norm_final):
    p = _prepare(norm_mix, w_in, conv_w, conv_b, dt_bias, a_log, d_skip, ssm_norm, w_attn_branch, w_ssm_branch,
                 w_out, norm_mlp, w_mlp_in, w_mlp_out, norm_final)
    return _trunk(x_prompt, p), _trunk(x_sample, p)
```

```python
import functools
import math

import jax
import jax.numpy as jnp
from jax import lax
from jax.experimental import pallas as pl
from jax.experimental.pallas import tpu as pltpu

F32 = jnp.float32
BF16 = jnp.bfloat16

D_MODEL = 1024
HEAD_DIM = 64
DILATIONS = (1, 4, 16)
RADIUS = 64
HEADS_PER_GROUP = 8
GROUP_WIDTH = HEADS_PER_GROUP * HEAD_DIM
QKV_WIDTH = len(DILATIONS) * GROUP_WIDTH
ROPE_THETA = 10000.0
D_INNER = 2048
SSM_GROUPS = 8
HEADS_PER_SSM_GROUP = 4
SSM_HEADS = SSM_GROUPS * HEADS_PER_SSM_GROUP
SSM_GROUP_WIDTH = HEADS_PER_SSM_GROUP * HEAD_DIM
D_STATE = 128
CONV_WIDTH = 5
CONV_DIM = D_INNER + 2 * SSM_GROUPS * D_STATE
CHUNK = 128
D_FF = 4096
RMS_EPS = 1e-6
NEG_INF = -1e30

LANES = 128
CONV_HALO = 64
CONV_ROWS = 32
LOG2_E = math.log2(math.e)
VMEM_LIMIT_BYTES = 56 * 1024 * 1024

QKV_TOKENS = 1024
SSMPROJ_TOKENS = 1024
MERGE_TOKENS = 512
MLP_TOKENS = 1024
ATTN_QUERY_TILE = {1: 512, 4: 256, 16: 128}
ATTN_QUERY_BLOCK = 128
MATMUL_COLS = 512


def _params(semantics):
    return pltpu.CompilerParams(dimension_semantics=semantics, vmem_limit_bytes=VMEM_LIMIT_BYTES)


def _resident(shape):
    zeros = (0,) * len(shape)
    return pl.BlockSpec(shape, lambda *_: zeros, pipeline_mode=pl.Buffered(1))


def _rms_scale(x, w):
    return x * lax.rsqrt(jnp.mean(x * x, axis=-1, keepdims=True) + RMS_EPS) * w


def _dot(a, b):
    return jnp.dot(a, b, preferred_element_type=F32)


def _dot_nt(a, b):
    return lax.dot_general(a, b, (((1,), (1,)), ((), ())), preferred_element_type=F32)


def _qkv_kernel(x_ref, nw_ref, w_ref, cos_ref, sin_ref, *refs):
    out_refs, stage = refs[:-1], refs[-1]
    tm = x_ref.shape[0]
    u = _rms_scale(x_ref[...], nw_ref[...]).astype(BF16)
    cos = cos_ref[...]
    sin = sin_ref[...]
    lane = lax.broadcasted_iota(jnp.int32, cos.shape, 1)
    first_half = (lane % HEAD_DIM) < (HEAD_DIM // 2)

    def rope(p):
        partner = jnp.where(first_half, pltpu.roll(p, LANES - HEAD_DIM // 2, axis=1),
                            pltpu.roll(p, HEAD_DIM // 2, axis=1))
        return p * cos + partner * sin

    for seg, (rotary, scale) in enumerate(((True, LOG2_E / math.sqrt(HEAD_DIM)), (True, 1.0), (False, 1.0))):
        for group, dilation in enumerate(DILATIONS):
            o_ref = out_refs[seg * len(DILATIONS) + group]
            col = seg * QKV_WIDTH + group * GROUP_WIDTH
            acc = _dot(u, w_ref[:, col:col + GROUP_WIDTH])
            for t in range(GROUP_WIDTH // LANES):
                piece = acc[:, t * LANES:(t + 1) * LANES]
                if rotary:
                    piece = rope(piece) * scale
                if dilation == 1:
                    o_ref[:, t * LANES:(t + 1) * LANES] = piece.astype(BF16)
                else:
                    stage[t] = piece
                    for r in range(dilation):
                        o_ref[:, r * GROUP_WIDTH + t * LANES:r * GROUP_WIDTH + (t + 1) * LANES] = (
                            stage[t, pl.ds(r, tm // dilation, stride=dilation), :].astype(BF16))


def _qkv_call(x2, nw, w_qkv, cos, sin, seq_len):
    tokens = x2.shape[0]
    tm = QKV_TOKENS
    pos_blocks = seq_len // tm
    tok = lambda i: (i, 0)
    pos = lambda i: (i % pos_blocks, 0)
    shapes, specs = [], []
    for _ in range(3):
        for d in DILATIONS:
            shapes.append(jax.ShapeDtypeStruct((tokens // d, d * GROUP_WIDTH), BF16))
            specs.append(pl.BlockSpec((tm // d, d * GROUP_WIDTH), tok))
    return pl.pallas_call(
        _qkv_kernel,
        out_shape=tuple(shapes),
        grid=(tokens // tm,),
        in_specs=[pl.BlockSpec((tm, D_MODEL), tok), _resident((1, D_MODEL)),
                  _resident((D_MODEL, 3 * QKV_WIDTH)),
                  pl.BlockSpec((tm, LANES), pos), pl.BlockSpec((tm, LANES), pos)],
        out_specs=tuple(specs),
        scratch_shapes=[pltpu.VMEM((GROUP_WIDTH // LANES, tm, LANES), F32)],
        compiler_params=_params(("parallel",)),
        name="qkv_proj",
    )(x2, nw, w_qkv, cos, sin)


def _softplus(x):
    return jnp.maximum(x, 0.0) + jnp.log1p(jnp.exp(-jnp.abs(x)))


def _ssmproj_kernel(x_ref, nw_ref, wz_ref, wx_ref, wdt_ref, dtb_ref, z_ref, xbc_ref, dt_ref):
    u32 = _rms_scale(x_ref[...], nw_ref[...])
    u = u32.astype(BF16)
    for c in range(D_INNER // MATMUL_COLS):
        sl = slice(c * MATMUL_COLS, (c + 1) * MATMUL_COLS)
        z_ref[:, sl] = _dot(u, wz_ref[:, sl]).astype(BF16)
    for c in range(CONV_DIM // MATMUL_COLS):
        sl = slice(c * MATMUL_COLS, (c + 1) * MATMUL_COLS)
        xbc_ref[:, sl] = _dot(u, wx_ref[:, sl]).astype(BF16)
    dt_t = lax.dot_general(wdt_ref[...], u32, (((1,), (1,)), ((), ())),
                           precision=lax.Precision.HIGHEST, preferred_element_type=F32)
    dt_t = _softplus(dt_t + dtb_ref[...])
    rows = 2 * HEADS_PER_SSM_GROUP
    for g in range(SSM_GROUPS):
        for c in range(dt_ref.shape[1] // rows):
            dt_ref[g, c * rows:(c + 1) * rows, :] = dt_t[g * rows:(g + 1) * rows, c * CHUNK:(c + 1) * CHUNK]


def _ssmproj_call(x2, nw, w_z, w_xbc, w_dt_t, dt_bias, batch, seq_len):
    tokens = x2.shape[0]
    tm = SSMPROJ_TOKENS
    per_seq = seq_len // tm
    rows = 2 * HEADS_PER_SSM_GROUP
    tok = lambda i: (i, 0)
    return pl.pallas_call(
        _ssmproj_kernel,
        out_shape=(jax.ShapeDtypeStruct((tokens, D_INNER), BF16),
                   jax.ShapeDtypeStruct((tokens, CONV_DIM), BF16),
                   jax.ShapeDtypeStruct((SSM_GROUPS, batch, seq_len // CHUNK * rows, CHUNK), F32)),
        grid=(tokens // tm,),
        in_specs=[pl.BlockSpec((tm, D_MODEL), tok), _resident((1, D_MODEL)),
                  _resident((D_MODEL, D_INNER)), _resident((D_MODEL, CONV_DIM)),
                  _resident((2 * SSM_HEADS, D_MODEL)), _resident((2 * SSM_HEADS, 1))],
        out_specs=(pl.BlockSpec((tm, D_INNER), tok), pl.BlockSpec((tm, CONV_DIM), tok),
                   pl.BlockSpec((SSM_GROUPS, None, tm // CHUNK * rows, CHUNK),
                                lambda i: (0, i // per_seq, i % per_seq, 0))),
        compiler_params=_params(("parallel",)),
        name="ssm_proj",
    )(x2, nw, w_z, w_xbc, w_dt_t, dt_bias)


def _attn_kernel(q_ref, kp_ref, km_ref, kn_ref, vp_ref, vm_ref, vn_ref, o_ref, lse_ref, o_s, k_s, v_s, *, tq,
                 sub_len, dilation):
    j = pl.program_id(1)
    r = pl.program_id(2)
    for dst, parts in ((k_s, (kp_ref, km_ref, kn_ref)), (v_s, (vp_ref, vm_ref, vn_ref))):
        dst[0:RADIUS] = parts[0][...]
        dst[RADIUS:RADIUS + tq] = parts[1][...]
        dst[RADIUS + tq:] = parts[2][...]
    qb = min(ATTN_QUERY_BLOCK, tq)
    kb = qb + 2 * RADIUS
    row = lax.broadcasted_iota(jnp.int32, (qb, kb), 0)
    col = lax.broadcasted_iota(jnp.int32, (qb, kb), 1)
    band = (col >= row) & (col <= row + 2 * RADIUS)
    head_lane = lax.broadcasted_iota(jnp.int32, (qb, LANES), 1)
    low_q = lax.broadcasted_iota(jnp.int32, (qb, LANES), 1) < HEAD_DIM

    def query_block(sb, carry):
        base = pl.multiple_of(sb * qb, qb)
        key_pos = j * tq + base - RADIUS + col
        valid = band & (key_pos >= 0) & (key_pos < sub_len)
        rows = pl.ds(r + base * dilation, qb, stride=dilation) if dilation > 1 else pl.ds(base, qb)
        lse_tile = jnp.zeros((qb, LANES), F32)
        for pair in range(HEADS_PER_GROUP // 2):
            sl = slice(pair * LANES, (pair + 1) * LANES)
            qp = q_ref[pl.ds(base, qb), sl]
            kp = k_s[pl.ds(base, kb), sl]
            vp = v_s[pl.ds(base, kb), sl]
            halves = []
            for half in range(2):
                keep_q = low_q if half == 0 else ~low_q
                s = jnp.where(valid, _dot_nt(jnp.where(keep_q, qp, 0), kp), NEG_INF)
                m = jnp.max(s, axis=-1, keepdims=True)
                p = jnp.exp2(s - m)
                l = jnp.sum(p, axis=-1, keepdims=True)
                halves.append(_dot(p.astype(BF16), vp) / l)
                lse_tile = jnp.where(head_lane == 2 * pair + half, (m + jnp.log2(l)) * (1.0 / LOG2_E), lse_tile)
            o_s[pair, rows, :] = jnp.where(low_q, halves[0], halves[1])
        lse_ref[rows, :] = lse_tile
        return carry

    lax.fori_loop(0, tq // qb, query_block, 0, unroll=True)

    @pl.when(r == dilation - 1)
    def _():
        for pair in range(HEADS_PER_GROUP // 2):
            o_ref[:, pair * LANES:(pair + 1) * LANES] = o_s[pair].astype(BF16)


def _attn_call(q, k, v, dilation, batch, seq_len):
    sub_len = seq_len // dilation
    tq = min(ATTN_QUERY_TILE[dilation], sub_len)
    halo_per_tile = tq // RADIUS
    last_halo = sub_len // RADIUS - 1
    view = lambda a: a.reshape(batch, sub_len, dilation * GROUP_WIDTH)
    main = lambda b, j, r: (b, j, r)
    prev = lambda b, j, r: (b, jnp.maximum(j * halo_per_tile - 1, 0), r)
    nxt = lambda b, j, r: (b, jnp.minimum((j + 1) * halo_per_tile, last_halo), r)
    out_map = lambda b, j, r: (b, j, 0)
    tile = lambda m: pl.BlockSpec((None, tq, GROUP_WIDTH), m)
    halo = lambda m: pl.BlockSpec((None, RADIUS, GROUP_WIDTH), m)
    qv, kv, vv = view(q), view(k), view(v)
    out, lse = pl.pallas_call(
        functools.partial(_attn_kernel, tq=tq, sub_len=sub_len, dilation=dilation),
        out_shape=(jax.ShapeDtypeStruct((batch, seq_len, GROUP_WIDTH), BF16),
                   jax.ShapeDtypeStruct((batch, seq_len, LANES), F32)),
        grid=(batch, sub_len // tq, dilation),
        in_specs=[tile(main), halo(prev), tile(main), halo(nxt), halo(prev), tile(main), halo(nxt)],
        out_specs=(pl.BlockSpec((None, tq * dilation, GROUP_WIDTH), out_map),
                   pl.BlockSpec((None, tq * dilation, LANES), out_map)),
        scratch_shapes=[pltpu.VMEM((GROUP_WIDTH // LANES, tq * dilation, LANES), F32),
                        pltpu.VMEM((tq + 2 * RADIUS, GROUP_WIDTH), BF16),
                        pltpu.VMEM((tq + 2 * RADIUS, GROUP_WIDTH), BF16)],
        compiler_params=_params(("parallel", "parallel", "arbitrary")),
        name=f"band_attn_d{dilation}",
    )(qv, kv, kv, kv, vv, vv, vv)
    return out.reshape(batch * seq_len, GROUP_WIDTH), lse.reshape(batch * seq_len, LANES)


def _ssd_kernel(xp_ref, bp_ref, cp_ref, dt_ref, cwx_ref, cwb_ref, cwc_ref, cbx_ref, cbb_ref, cbc_ref,
                alog_ref, dsk_ref, y_ref, xs_s, bm_s, cm_s, bt_s, hb_s, hf_s, hbc_s, acs_s, inj_s, dec_s, cols_s, shift_s, *,
                seq_len):
    L = CHUNK
    nc = seq_len // L
    heads = HEADS_PER_SSM_GROUP
    pair_w = 2 * HEAD_DIM

    side_taps = [t for t in range(CONV_WIDTH) if t != CONV_WIDTH // 2]
    sel_row = lax.broadcasted_iota(jnp.int32, (len(side_taps) * L, 2 * L), 0)
    sel_col = lax.broadcasted_iota(jnp.int32, (len(side_taps) * L, 2 * L), 1)
    slab = len(side_taps) * CONV_ROWS
    tap_slot = sel_row % slab // CONV_ROWS
    tap_offset = jnp.where(tap_slot < CONV_WIDTH // 2, tap_slot, tap_slot + 1) - CONV_WIDTH // 2
    out_row = sel_row // slab * CONV_ROWS + sel_row % CONV_ROWS
    shift_s[...] = (sel_col == CONV_HALO + out_row + tap_offset).astype(BF16)

    def conv_chunk(c, carry):
        r0 = pl.multiple_of(c * L, L)
        lo = pl.multiple_of(jnp.maximum(r0 - CONV_HALO, 0), CONV_HALO)
        hi = pl.multiple_of(jnp.minimum(r0 + L, seq_len - CONV_HALO), CONV_HALO)
        shifts = shift_s[...]

        def window(rows, size, keep):
            parts = [src[pl.ds(rows, size), :] for src in (xp_ref, bp_ref, cp_ref)]
            parts = [jnp.where(keep, part, jnp.zeros_like(part)) for part in parts]
            return parts[0], jnp.concatenate(parts[1:], axis=1)

        windows = (window(lo, CONV_HALO, c > 0), window(r0, L, True), window(hi, CONV_HALO, c < nc - 1))
        weights = (cwx_ref[...], jnp.concatenate([cwb_ref[...], cwc_ref[...]], axis=1))
        biases = (cbx_ref[...], jnp.concatenate([cbb_ref[...], cbc_ref[...]], axis=1))
        acts = []
        for half in range(2):
            before, cur, after = (windows[part][half] for part in range(3))
            shifted = _dot(shifts, jnp.concatenate([before, cur, after], axis=0))
            w = weights[half]
            blocks = []
            for blk in range(L // CONV_ROWS):
                rows = slice(blk * CONV_ROWS, (blk + 1) * CONV_ROWS)
                acc = biases[half] + cur[rows].astype(F32) * w[CONV_WIDTH // 2:CONV_WIDTH // 2 + 1]
                for slot, tap in enumerate(side_taps):
                    start = (blk * len(side_taps) + slot) * CONV_ROWS
                    acc = acc + shifted[start:start + CONV_ROWS] * w[tap:tap + 1]
                blocks.append(acc * jax.nn.sigmoid(acc))
            acts.append(jnp.concatenate(blocks, axis=0))
        xs_s[pl.ds(r0, L), :] = acts[0].astype(BF16)
        bm_s[pl.ds(r0, L), :] = acts[1][:, :D_STATE].astype(BF16)
        cm_s[pl.ds(r0, L), :] = acts[1][:, D_STATE:].astype(BF16)
        bt_s[c] = jnp.transpose(acts[1][:, :D_STATE]).astype(BF16)
        return carry

    rows8 = 2 * heads
    dt_all = dt_ref[...]
    a_rows = -jnp.exp(alog_ref[...]) * LOG2_E
    dta = dt_all * jnp.concatenate([a_rows] * nc, axis=0)
    fwd_rows = lax.broadcasted_iota(jnp.int32, dta.shape, 0) % rows8 < heads
    ones_upper = (lax.broadcasted_iota(jnp.int32, (L, L), 0) <= lax.broadcasted_iota(jnp.int32, (L, L), 1)).astype(BF16)
    cs = jnp.zeros(dta.shape, F32)
    rest = dta
    for _ in range(3):
        piece = rest.astype(BF16)
        cs = cs + _dot(piece, ones_upper)
        rest = rest - piece.astype(F32)
    total = cs[:, L - 1:L]
    acs_all = jnp.where(fwd_rows, cs, total - cs + dta)
    acs_s[...] = acs_all
    inj_s[...] = dt_all * jnp.exp2(total - acs_all)
    dec_s[...] = jnp.broadcast_to(jnp.exp2(total), dta.shape)
    pad_rows = jnp.zeros((L - rows8, L), F32)

    def column_chunk(c, carry):
        acs = acs_s[pl.ds(pl.multiple_of(c * rows8, rows8), rows8), :]
        cols_s[c] = jnp.transpose(jnp.concatenate([acs, pad_rows], axis=0))
        return carry

    def prepass(c, carry):
        return column_chunk(c, conv_chunk(c, carry))

    lax.fori_loop(0, nc, prepass, 0, unroll=8)

    hf_s[...] = jnp.zeros_like(hf_s)
    hbc_s[...] = jnp.zeros_like(hbc_s)
    dskip = dsk_ref[...]
    row = lax.broadcasted_iota(jnp.int32, (L, L), 0)
    col = lax.broadcasted_iota(jnp.int32, (L, L), 1)
    lower, strict_lower, strict_upper = row >= col, row > col, row < col
    low_lanes = lax.broadcasted_iota(jnp.int32, (L, pair_w), 1) < HEAD_DIM
    state_lane = lax.broadcasted_iota(jnp.int32, (1, pair_w), 1) < HEAD_DIM

    def decays(c):
        rows = pl.ds(pl.multiple_of(c * rows8, rows8), rows8)
        return dt_ref[rows, :], acs_s[rows, :], dec_s[rows, 0:1], inj_s[rows, :]

    def split_heads(xp):
        return jnp.concatenate([jnp.where(low_lanes, xp, 0), jnp.where(low_lanes, 0, xp)], axis=0)

    def state_update(h_s, c, r0, chunk_decay, inject, base):
        bt = bt_s[c].astype(F32)
        for pr in range(heads // 2):
            sl = slice(pr * pair_w, (pr + 1) * pair_w)
            hr = base + 2 * pr
            lhs = jnp.concatenate([(bt * inject[hr:hr + 1]).astype(BF16),
                                   (bt * inject[hr + 1:hr + 2]).astype(BF16)], axis=1)
            st = _dot(lhs, split_heads(xs_s[pl.ds(r0, L), sl]))
            dec = jnp.where(state_lane, chunk_decay[hr:hr + 1], chunk_decay[hr + 1:hr + 2])
            h_s[:, sl] = h_s[:, sl] * dec + st

    def sweep_a(i, carry):
        c = nc - 1 - i
        r0 = pl.multiple_of(c * L, L)
        _, _, chunk_decay, inject = decays(c)
        hb_s[c] = hbc_s[...].astype(BF16)
        state_update(hbc_s, c, r0, chunk_decay, inject, heads)
        return carry

    def sweep_b(c, carry):
        r0 = pl.multiple_of(c * L, L)
        dt8, acs, chunk_decay, inject = decays(c)
        cols = cols_s[c]
        cm = cm_s[pl.ds(r0, L), :]
        cb = _dot_nt(cm, bm_s[pl.ds(r0, L), :])
        from_fwd = _dot(cm, hf_s[...].astype(BF16))
        from_bwd = _dot(cm, hb_s[c])
        for pr in range(heads // 2):
            sl = slice(pr * pair_w, (pr + 1) * pair_w)
            w_sum, col_f, col_b = [], [], []
            for e in (2 * pr, 2 * pr + 1):
                fwd, bwd = slice(e, e + 1), slice(heads + e, heads + e + 1)
                col_f.append(jnp.broadcast_to(cols[:, fwd], (L, L)))
                col_b.append(jnp.broadcast_to(cols[:, bwd], (L, L)))
                decay = jnp.exp2(jnp.where(lower, col_f[-1] - acs[fwd], col_b[-1] - acs[bwd]))
                dt_sel = jnp.where(strict_lower, dt8[fwd], jnp.where(strict_upper, dt8[bwd], dt8[fwd] + dt8[bwd]))
                w_sum.append((cb * (decay * dt_sel)).astype(BF16))
            xp = xs_s[pl.ds(r0, L), sl]
            y = _dot(jnp.concatenate(w_sum, axis=1), split_heads(xp))
            y = y + from_fwd[:, sl] * jnp.exp2(jnp.where(low_lanes, col_f[0], col_f[1]))
            y = y + from_bwd[:, sl] * jnp.exp2(jnp.where(low_lanes, col_b[0], col_b[1]))
            y = y + xp.astype(F32) * dskip[:, sl]
            y_ref[pl.ds(r0, L), sl] = y.astype(BF16)
        state_update(hf_s, c, r0, chunk_decay, inject, 0)
        return carry

    lax.fori_loop(0, nc, sweep_a, 0, unroll=4)
    lax.fori_loop(0, nc, sweep_b, 0, unroll=8)


def _ssd_call(xbc, dt, conv_w, conv_b, a_log, d_skip, batch, seq_len):
    b_off = D_INNER // D_STATE
    c_off = b_off + SSM_GROUPS
    nc = seq_len // CHUNK
    seq_block = lambda w, m: pl.BlockSpec((None, seq_len, w), m)
    rows = 2 * HEADS_PER_SSM_GROUP
    return pl.pallas_call(
        functools.partial(_ssd_kernel, seq_len=seq_len),
        out_shape=jax.ShapeDtypeStruct((batch, seq_len, D_INNER), BF16),
        grid=(batch, SSM_GROUPS),
        in_specs=[seq_block(SSM_GROUP_WIDTH, lambda b, g: (b, 0, g)),
                  seq_block(D_STATE, lambda b, g: (b, 0, b_off + g)),
                  seq_block(D_STATE, lambda b, g: (b, 0, c_off + g)),
                  pl.BlockSpec((None, None, nc * rows, CHUNK), lambda b, g: (g, b, 0, 0)),
                  pl.BlockSpec((CONV_WIDTH, SSM_GROUP_WIDTH), lambda b, g: (0, g)),
                  pl.BlockSpec((CONV_WIDTH, D_STATE), lambda b, g: (0, b_off + g)),
                  pl.BlockSpec((CONV_WIDTH, D_STATE), lambda b, g: (0, c_off + g)),
                  pl.BlockSpec((1, SSM_GROUP_WIDTH), lambda b, g: (0, g)),
                  pl.BlockSpec((1, D_STATE), lambda b, g: (0, b_off + g)),
                  pl.BlockSpec((1, D_STATE), lambda b, g: (0, c_off + g)),
                  pl.BlockSpec((None, rows, 1), lambda b, g: (g, 0, 0)),
                  pl.BlockSpec((None, 1, SSM_GROUP_WIDTH), lambda b, g: (g, 0, 0))],
        out_specs=seq_block(SSM_GROUP_WIDTH, lambda b, g: (b, 0, g)),
        scratch_shapes=[pltpu.VMEM((seq_len, SSM_GROUP_WIDTH), BF16), pltpu.VMEM((seq_len, D_STATE), BF16),
                        pltpu.VMEM((seq_len, D_STATE), BF16), pltpu.VMEM((nc, D_STATE, CHUNK), BF16),
                        pltpu.VMEM((nc, D_STATE, SSM_GROUP_WIDTH), BF16),
                        pltpu.VMEM((D_STATE, SSM_GROUP_WIDTH), F32), pltpu.VMEM((D_STATE, SSM_GROUP_WIDTH), F32),
                        pltpu.VMEM((nc * rows, CHUNK), F32), pltpu.VMEM((nc * rows, CHUNK), F32),
                        pltpu.VMEM((nc * rows, CHUNK), F32), pltpu.VMEM((nc, CHUNK, LANES), F32),
                        pltpu.VMEM(((CONV_WIDTH - 1) * CHUNK, 2 * CHUNK), BF16)],
        compiler_params=_params(("parallel", "parallel")),
        name="conv_ssd",
    )(xbc, xbc, xbc, dt, conv_w, conv_w, conv_w, conv_b, conv_b, conv_b, a_log, d_skip)


def _merge_kernel(x_ref, o0_ref, o1_ref, o2_ref, l0_ref, l1_ref, l2_ref, y_ref, z_ref, nw_ref, sn_ref,
                  wga_ref, wgm_ref, wab_ref, wsb_ref, wo_ref, ex_ref, nm_ref, w1_ref, w2_ref, nf_ref, h_ref):
    x = x_ref[...]
    u = _rms_scale(x, nw_ref[...]).astype(BF16)
    lses = (l0_ref[...], l1_ref[...], l2_ref[...])
    top = jnp.maximum(jnp.maximum(lses[0], lses[1]), lses[2])
    es = [jnp.exp(l - top) for l in lses]
    inv = 1.0 / (es[0] + es[1] + es[2])
    expand = ex_ref[...]
    mix = jnp.zeros((x.shape[0], GROUP_WIDTH), F32)
    for e, o_ref in zip(es, (o0_ref, o1_ref, o2_ref)):
        alpha = e * inv
        hi = alpha.astype(BF16)
        lo = (alpha - hi.astype(F32)).astype(BF16)
        mix = mix + (_dot(hi, expand) + _dot(lo, expand)) * o_ref[...].astype(F32)
    o_attn = _dot(mix.astype(BF16), wab_ref[...])
    z = z_ref[...].astype(F32)
    yz = y_ref[...].astype(F32) * (z * jax.nn.sigmoid(z))
    o_ssm = _dot(_rms_scale(yz, sn_ref[...]).astype(BF16), wsb_ref[...])
    merged = jax.nn.sigmoid(_dot(u, wga_ref[...])) * o_attn + jax.nn.sigmoid(_dot(u, wgm_ref[...])) * o_ssm
    h = x + _dot(merged.astype(BF16), wo_ref[...])
    u2 = _rms_scale(h, nm_ref[...]).astype(BF16)
    acc = h
    for c in range(D_FF // MATMUL_COLS):
        sl = slice(c * MATMUL_COLS, (c + 1) * MATMUL_COLS)
        a = jnp.maximum(_dot(u2, w1_ref[:, sl]), 0.0)
        acc = acc + _dot((a * a).astype(BF16), w2_ref[sl, :])
    h_ref[...] = _rms_scale(acc, nf_ref[...])


def _merge_call(x2, outs, lses, y, z, nw, ssm_norm, w_ga, w_gm, w_ab, w_sb, w_out, expand, nm, w1, w2, nf):
    tokens = x2.shape[0]
    tm = MERGE_TOKENS
    tok = lambda w: pl.BlockSpec((tm, w), lambda i: (i, 0))
    return pl.pallas_call(
        _merge_kernel,
        out_shape=jax.ShapeDtypeStruct((tokens, D_MODEL), F32),
        grid=(tokens // tm,),
        in_specs=[tok(D_MODEL), tok(GROUP_WIDTH), tok(GROUP_WIDTH), tok(GROUP_WIDTH),
                  tok(LANES), tok(LANES), tok(LANES), tok(D_INNER), tok(D_INNER),
                  _resident((1, D_MODEL)), _resident((1, D_INNER)),
                  _resident((D_MODEL, D_MODEL)), _resident((D_MODEL, D_MODEL)),
                  _resident((GROUP_WIDTH, D_MODEL)), _resident((D_INNER, D_MODEL)),
                  _resident((D_MODEL, D_MODEL)), _resident((LANES, GROUP_WIDTH)),
                  _resident((1, D_MODEL)), _resident((D_MODEL, D_FF)), _resident((D_FF, D_MODEL)),
                  _resident((1, D_MODEL))],
        out_specs=tok(D_MODEL),
        compiler_params=_params(("parallel",)),
        name="merge_out",
    )(x2, *outs, *lses, y, z, nw, ssm_norm, w_ga, w_gm, w_ab, w_sb, w_out, expand, nm, w1, w2, nf)


def _mlp_kernel(h_ref, nw_ref, w1_ref, w2_ref, nf_ref, o_ref):
    h = h_ref[...]
    u = _rms_scale(h, nw_ref[...]).astype(BF16)
    acc = h
    for c in range(D_FF // MATMUL_COLS):
        sl = slice(c * MATMUL_COLS, (c + 1) * MATMUL_COLS)
        a = jnp.maximum(_dot(u, w1_ref[:, sl]), 0.0)
        acc = acc + _dot((a * a).astype(BF16), w2_ref[sl, :])
    o_ref[...] = _rms_scale(acc, nf_ref[...])


def _mlp_call(h, nw, w1, w2, nf):
    tokens = h.shape[0]
    tm = MLP_TOKENS
    tok = pl.BlockSpec((tm, D_MODEL), lambda i: (i, 0))
    return pl.pallas_call(
        _mlp_kernel,
        out_shape=jax.ShapeDtypeStruct((tokens, D_MODEL), F32),
        grid=(tokens // tm,),
        in_specs=[tok, _resident((1, D_MODEL)), _resident((D_MODEL, D_FF)), _resident((D_FF, D_MODEL)),
                  _resident((1, D_MODEL))],
        out_specs=tok,
        compiler_params=_params(("parallel",)),
        name="mlp_norm",
    )(h, nw, w1, w2, nf)


def _rope_tables(seq_len):
    half = HEAD_DIM // 2
    inv = ROPE_THETA ** (-(jnp.arange(half, dtype=F32) * 2.0 / HEAD_DIM))
    ang = jnp.arange(seq_len, dtype=F32)[:, None] * inv[None, :]
    cos, sin = jnp.cos(ang), jnp.sin(ang)
    reps = LANES // HEAD_DIM
    return (jnp.tile(jnp.concatenate([cos, cos], axis=-1), (1, reps)),
            jnp.tile(jnp.concatenate([-sin, sin], axis=-1), (1, reps)))


def _trunk(x, p):
    batch, seq_len, _ = x.shape
    x2 = x.reshape(batch * seq_len, D_MODEL)
    cos, sin = _rope_tables(seq_len)
    qkv = _qkv_call(x2, p["norm_mix"], p["w_qkv"], cos, sin, seq_len)
    z, xbc, dt = _ssmproj_call(x2, p["norm_mix"], p["w_z"], p["w_xbc"], p["w_dt_t"], p["dt_bias"], batch, seq_len)
    outs, lses = [], []
    groups = len(DILATIONS)
    for group, dilation in enumerate(DILATIONS):
        o, lse = _attn_call(qkv[group], qkv[groups + group], qkv[2 * groups + group], dilation, batch, seq_len)
        outs.append(o)
        lses.append(lse)
    y = _ssd_call(xbc.reshape(batch, seq_len, CONV_DIM), dt, p["conv_w"], p["conv_b"], p["a_log"], p["d_skip"],
                  batch, seq_len)
    h = _merge_call(x2, outs, lses, y.reshape(batch * seq_len, D_INNER), z, p["norm_mix"], p["ssm_norm"],
                    p["w_ga"], p["w_gm"], p["w_ab"], p["w_sb"], p["w_out"], p["expand"],
                    p["norm_mlp"], p["w_mlp_in"], p["w_mlp_out"], p["norm_final"])
    return h.reshape(batch, seq_len, D_MODEL)


def _prepare(norm_mix, w_in, conv_w, conv_b, dt_bias, a_log, d_skip, ssm_norm, w_attn_branch, w_ssm_branch,
             w_out, norm_mlp, w_mlp_in, w_mlp_out, norm_final):
    w = w_in[0]
    bounds = {}
    start = 0
    for name, width in (("qkv", 3 * QKV_WIDTH), ("z", D_INNER), ("xbc", CONV_DIM), ("dt", 2 * SSM_HEADS),
                        ("ga", D_MODEL), ("gm", D_MODEL)):
        bounds[name] = (start, start + width)
        start += width
    cut = lambda name: w[:, bounds[name][0]:bounds[name][1]]
    perm = jnp.arange(2 * SSM_HEADS).reshape(2, SSM_GROUPS, HEADS_PER_SSM_GROUP).transpose(1, 0, 2).reshape(-1)
    heads_lanes = jnp.arange(LANES)[:, None] == (jnp.arange(GROUP_WIDTH)[None, :] // HEAD_DIM)
    return {
        "norm_mix": norm_mix[0][None, :],
        "w_qkv": cut("qkv").astype(BF16),
        "w_z": cut("z").astype(BF16),
        "w_xbc": cut("xbc").astype(BF16),
        "w_dt_t": cut("dt")[:, perm].T,
        "dt_bias": dt_bias[0].reshape(-1)[perm][:, None],
        "a_log": a_log[0].reshape(-1)[perm].reshape(SSM_GROUPS, 2 * HEADS_PER_SSM_GROUP, 1),
        "d_skip": jnp.repeat(d_skip[0].reshape(SSM_GROUPS, HEADS_PER_SSM_GROUP), HEAD_DIM, axis=1)[:, None, :],
        "conv_w": conv_w[0],
        "conv_b": conv_b[0][None, :],
        "ssm_norm": ssm_norm[0][None, :],
        "w_ga": cut("ga").astype(BF16),
        "w_gm": cut("gm").astype(BF16),
        "w_ab": w_attn_branch[0].astype(BF16),
        "w_sb": w_ssm_branch[0].astype(BF16),
        "w_out": w_out[0].astype(BF16),
        "expand": heads_lanes.astype(BF16),
        "norm_mlp": norm_mlp[0][None, :],
        "w_mlp_in": w_mlp_in[0].astype(BF16),
        "w_mlp_out": w_mlp_out[0].astype(BF16),
        "norm_final": norm_final[None, :],
    }


def kernel(x_prompt, x_sample, norm_mix, w_in, conv_w, conv_b, dt_bias, a_log, d_skip, ssm_norm, w_attn_branch,
           w_ssm_branch, w_out, norm_mlp, w_mlp_in, w_mlp_out, norm_final):
    p = _prepare(norm_mix, w_in, conv_w, conv_b, dt_bias, a_log, d_skip, ssm_norm, w_attn_branch, w_ssm_branch,
                 w_out, norm_mlp, w_mlp_in, w_mlp_out, norm_final)
    return _trunk(x_prompt, p), _trunk(x_sample, p)
```
